```python
import jax, jax.numpy as jnp
from jax import lax
import numpy as np

D_MODEL = 4096
BATCH = 4
SEQ = 4096
DEPTH = 1
DEC_BATCH = 4
DEC_SEQ = 2048
PAST_LEN = 128

N_HEADS = 16
N_KV_HEADS = 4
HEAD_DIM = 128
Q_BLOCK = 128
ROPE_THETA = 10000.0
GRID_W = 64
FOURIER_CH = 2048
ATTN_W = N_HEADS * HEAD_DIM
KV_W = N_KV_HEADS * HEAD_DIM
IN_W = ATTN_W + 2 * KV_W + FOURIER_CH
N_BRANCH = 2
MEM_LEN = 256
CA_HEADS = 4
CA_HEAD_DIM = 256
CA_W = CA_HEADS * CA_HEAD_DIM
N_KEYS = 128
N_EXPERTS = N_KEYS * N_KEYS
PEER_HEADS = 8
PEER_TOPK = 16
PEER_QDIM = 256
PEER_HALF = PEER_QDIM // 2
PEER_TOK_BLOCK = 32
EPS = 1e-6

kernel_name = 'hybrid_fourier_gqa_peer_encoder'

F32 = jnp.float32


def rmsnorm(x, g):
    xf = x.astype(F32)
    y = xf * lax.rsqrt(jnp.mean(xf * xf, axis=-1, keepdims=True) + EPS)
    return (y * g.astype(F32)).astype(x.dtype)


def axial_rope_tables(seq_len):
    rows = seq_len // GRID_W
    row = jnp.repeat(jnp.arange(rows), GRID_W)
    col = jnp.tile(jnp.arange(GRID_W), rows)
    half = HEAD_DIM // 2
    inv_freq = ROPE_THETA ** (-jnp.arange(0, half, 2, dtype=F32) / half)
    ang = jnp.stack([row, col], axis=-1).astype(F32)[:, :, None] * inv_freq
    return jnp.cos(ang), jnp.sin(ang)


def apply_axial_rope(x, cos, sin):
    B, S, H, _ = x.shape
    xf = x.astype(F32).reshape(B, S, H, 2, 2, HEAD_DIM // 4)
    x1, x2 = xf[..., 0, :], xf[..., 1, :]
    c = cos[None, :, None]
    s = sin[None, :, None]
    out = jnp.stack([x1 * c - x2 * s, x2 * c + x1 * s], axis=-2)
    return out.reshape(B, S, H, HEAD_DIM).astype(x.dtype)


def gqa_attention(q, k, v):
    B, S, _, _ = q.shape
    G = N_HEADS // N_KV_HEADS
    nblk = S // Q_BLOCK
    qb = q.reshape(B, nblk, Q_BLOCK, N_KV_HEADS, G, HEAD_DIM).transpose(1, 0, 2, 3, 4, 5)
    scale = HEAD_DIM ** -0.5

    def one_block(qi):
        s = jnp.einsum('bqkgd,bskd->bkgqs', qi, k, preferred_element_type=F32) * scale
        p = jax.nn.softmax(s, axis=-1).astype(v.dtype)
        return jnp.einsum('bkgqs,bskd->bqkgd', p, v)

    o = lax.map(one_block, qb)
    return o.transpose(1, 0, 2, 3, 4, 5).reshape(B, S, ATTN_W)


def fourier_mix(u):
    z = jnp.fft.fft2(u.astype(F32), axes=(1, 2), norm='ortho')
    return jnp.real(z).astype(u.dtype)


def memory_cross_attention(h, mem, mem_norm, w_cq, w_ckv, w_co):
    B, S, _ = h.shape
    M = mem.shape[1]
    m = rmsnorm(mem, mem_norm)
    q = (h @ w_cq).reshape(B, S, CA_HEADS, CA_HEAD_DIM)
    kv = (m @ w_ckv).reshape(B, M, 2, CA_HEADS, CA_HEAD_DIM)
    k, v = kv[:, :, 0], kv[:, :, 1]
    s = jnp.einsum('bqhd,bmhd->bhqm', q, k, preferred_element_type=F32) * (CA_HEAD_DIM ** -0.5)
    p = jax.nn.softmax(s, axis=-1).astype(v.dtype)
    o = jnp.einsum('bhqm,bmhd->bqhd', p, v).reshape(B, S, CA_W)
    return o @ w_co


def peer(h, w_pq, sub_keys, expert_u, expert_v):
    B, S, D = h.shape
    q = (h @ w_pq).reshape(B, S, PEER_HEADS, 2, PEER_HALF).astype(F32)
    sk = jnp.einsum('bshcd,cnd->bshcn', q, sub_keys.astype(F32))
    s_top, i_top = lax.top_k(sk, PEER_TOPK)
    cand_s = (s_top[..., 0, :, None] + s_top[..., 1, None, :]).reshape(B, S, PEER_HEADS, PEER_TOPK * PEER_TOPK)
    cand_i = (i_top[..., 0, :, None] * N_KEYS + i_top[..., 1, None, :]).reshape(B, S, PEER_HEADS, PEER_TOPK * PEER_TOPK)
    best_s, best_pos = lax.top_k(cand_s, PEER_TOPK)
    ids = jnp.take_along_axis(cand_i, best_pos, axis=-1)
    gates = jax.nn.softmax(best_s, axis=-1)
    T = B * S
    nb = T // PEER_TOK_BLOCK
    E = PEER_HEADS * PEER_TOPK
    hb = h.reshape(nb, PEER_TOK_BLOCK, D)
    ib = ids.reshape(nb, PEER_TOK_BLOCK, E)
    gb = gates.reshape(nb, PEER_TOK_BLOCK, E).astype(h.dtype)

    def block(args):
        hx, ix, gx = args
        u = expert_u[ix]
        a = jax.nn.gelu(jnp.einsum('td,ted->te', hx, u), approximate=False)
        v = expert_v[ix]
        return jnp.einsum('te,ted->td', gx * a, v)

    out = lax.map(block, (hb, ib, gb))
    return out.reshape(B, S, D)


def encoder_layer(x, mem, norm_mix, w_in, q_norm, k_norm, w_attn_br, w_four_br, w_gate, b_gate, w_out,
                  norm_ca, mem_norm, w_cq, w_ckv, w_co, norm_ffn, w_pq, sub_keys, expert_u, expert_v):
    B, S, D = x.shape
    h = rmsnorm(x, norm_mix)
    proj = h @ w_in
    q, k, v, f = jnp.split(proj, [ATTN_W, ATTN_W + KV_W, ATTN_W + 2 * KV_W], axis=-1)
    q = rmsnorm(q.reshape(B, S, N_HEADS, HEAD_DIM), q_norm)
    k = rmsnorm(k.reshape(B, S, N_KV_HEADS, HEAD_DIM), k_norm)
    v = v.reshape(B, S, N_KV_HEADS, HEAD_DIM)
    cos, sin = axial_rope_tables(S)
    q = apply_axial_rope(q, cos, sin)
    k = apply_axial_rope(k, cos, sin)
    attn_br = gqa_attention(q, k, v) @ w_attn_br
    four_br = fourier_mix(f) @ w_four_br
    g = jax.nn.sigmoid((h @ w_gate + b_gate).astype(F32)).astype(x.dtype).reshape(B, S, N_BRANCH, D)
    merged = g[:, :, 0] * attn_br + g[:, :, 1] * four_br
    x = x + merged @ w_out
    x = x + memory_cross_attention(rmsnorm(x, norm_ca), mem, mem_norm, w_cq, w_ckv, w_co)
    x = x + peer(rmsnorm(x, norm_ffn), w_pq, sub_keys, expert_u, expert_v)
    return x


def trunk(x, mem, layer_params, final_norm):
    (norm_mix, w_in, q_norm, k_norm, w_attn_br, w_four_br, w_gate, b_gate, w_out,
     norm_ca, mem_norm, w_cq, w_ckv, w_co, norm_ffn, w_pq, sub_keys, expert_u, expert_v) = layer_params
    for l in range(DEPTH):
        x = encoder_layer(x, mem, norm_mix[l], w_in[l], q_norm[l], k_norm[l], w_attn_br[l], w_four_br[l],
                          w_gate[l], b_gate[l], w_out[l], norm_ca[l], mem_norm[l], w_cq[l], w_ckv[l], w_co[l],
                          norm_ffn[l], w_pq[l], sub_keys[l], expert_u[l], expert_v[l])
    return rmsnorm(x, final_norm)


def setup_inputs(seed: int = 0) -> dict:
    key = jax.random.key(seed)
    ks = jax.random.split(key, 26)
    L, D = DEPTH, D_MODEL

    def nrm(k, shape, scale):
        return jax.random.normal(k, shape, F32) * scale

    def gain(k, shape):
        return 1.0 + 0.02 * jax.random.normal(k, shape, F32)

    return {
        'x_prompt': nrm(ks[0], (BATCH, SEQ, D), 1.0),
        'x_sample': nrm(ks[1], (DEC_BATCH, DEC_SEQ, D), 1.0),
        'mem_prompt': nrm(ks[2], (BATCH, MEM_LEN, D), 1.0),
        'mem_sample': nrm(ks[3], (DEC_BATCH, MEM_LEN, D), 1.0),
        'norm_mix': gain(ks[4], (L, D)),
        'w_in': nrm(ks[5], (L, D, IN_W), D ** -0.5),
        'q_norm': gain(ks[6], (L, HEAD_DIM)),
        'k_norm': gain(ks[7], (L, HEAD_DIM)),
        'w_attn_br': nrm(ks[8], (L, ATTN_W, D), ATTN_W ** -0.5),
        'w_four_br': nrm(ks[9], (L, FOURIER_CH, D), FOURIER_CH ** -0.5),
        'w_gate': nrm(ks[10], (L, D, N_BRANCH * D), D ** -0.5),
        'b_gate': nrm(ks[11], (L, N_BRANCH * D), 0.02),
        'w_out': nrm(ks[12], (L, D, D), D ** -0.5),
        'norm_ca': gain(ks[13], (L, D)),
        'mem_norm': gain(ks[14], (L, D)),
        'w_cq': nrm(ks[15], (L, D, CA_W), D ** -0.5),
        'w_ckv': nrm(ks[16], (L, D, 2 * CA_W), D ** -0.5),
        'w_co': nrm(ks[17], (L, CA_W, D), CA_W ** -0.5),
        'norm_ffn': gain(ks[18], (L, D)),
        'w_pq': nrm(ks[19], (L, D, PEER_HEADS * PEER_QDIM), D ** -0.5),
        'sub_keys': nrm(ks[20], (L, 2, N_KEYS, PEER_HALF), PEER_HALF ** -0.5),
        'expert_u': nrm(ks[21], (L, N_EXPERTS, D), D ** -0.5),
        'expert_v': nrm(ks[22], (L, N_EXPERTS, D), PEER_HEADS ** -0.5),
        'final_norm': gain(ks[23], (D,)),
    }


def reference(x_prompt, x_sample, mem_prompt, mem_sample, norm_mix, w_in, q_norm, k_norm, w_attn_br,
              w_four_br, w_gate, b_gate, w_out, norm_ca, mem_norm, w_cq, w_ckv, w_co, norm_ffn, w_pq,
              sub_keys, expert_u, expert_v, final_norm):
    layer_params = (norm_mix, w_in, q_norm, k_norm, w_attn_br, w_four_br, w_gate, b_gate, w_out,
                    norm_ca, mem_norm, w_cq, w_ckv, w_co, norm_ffn, w_pq, sub_keys, expert_u, expert_v)
    y_prompt = trunk(x_prompt, mem_prompt, layer_params, final_norm)
    y_sample = trunk(x_sample, mem_sample, layer_params, final_norm)
    return (y_prompt, y_sample)
```

```python
import functools
import math

import jax
import jax.numpy as jnp
from jax import lax
from jax.experimental import pallas as pl
from jax.experimental.pallas import tpu as pltpu

F32 = jnp.float32
BF16 = jnp.bfloat16

N_HEADS = 16
N_KV_HEADS = 4
HEAD_DIM = 128
GQA_GROUP = N_HEADS // N_KV_HEADS
ROPE_THETA = 10000.0
GRID_W = 64
CA_HEADS = 4
CA_HEAD_DIM = 256
N_KEYS = 128
PEER_HEADS = 8
PEER_TOPK = 16
PEER_HALF = 128
EPS = 1e-6
INV_SQRT2 = 0.7071067811865476

V7X_VMEM_BYTES = 64 * 1024 * 1024
VMEM_LIMIT = V7X_VMEM_BYTES - 8 * 1024 * 1024
LANES = 128


def _tile(dim, pref):
    t = min(pref, dim)
    while dim % t:
        t //= 2
    return t


def _params(*sem):
    return pltpu.CompilerParams(dimension_semantics=sem, vmem_limit_bytes=VMEM_LIMIT)


def _rmsnorm_body(x_ref, g_ref, o_ref):
    x = x_ref[...].astype(F32)
    ms = jnp.mean(x * x, axis=-1, keepdims=True)
    o_ref[...] = (x * lax.rsqrt(ms + EPS) * g_ref[...]).astype(o_ref.dtype)


def rmsnorm_rows(x2d, gain, out_dtype, tm=256):
    T, D = x2d.shape
    tm = _tile(T, tm)
    return pl.pallas_call(
        _rmsnorm_body,
        grid=(T // tm,),
        in_specs=[pl.BlockSpec((tm, D), lambda i: (i, 0)),
                  pl.BlockSpec((1, D), lambda i: (0, 0))],
        out_specs=pl.BlockSpec((tm, D), lambda i: (i, 0)),
        out_shape=jax.ShapeDtypeStruct((T, D), out_dtype),
        compiler_params=_params("parallel"),
        name="rmsnorm_rows",
    )(x2d, gain.reshape(1, D).astype(F32))


def _add_rmsnorm_body(x_ref, y_ref, g_ref, o_ref):
    x = x_ref[...].astype(F32) + y_ref[...].astype(F32)
    ms = jnp.mean(x * x, axis=-1, keepdims=True)
    o_ref[...] = (x * lax.rsqrt(ms + EPS) * g_ref[...]).astype(o_ref.dtype)


def add_rmsnorm_rows(x2d, y2d, gain, out_dtype, tm=256):
    T, D = x2d.shape
    tm = _tile(T, tm)
    return pl.pallas_call(
        _add_rmsnorm_body,
        grid=(T // tm,),
        in_specs=[pl.BlockSpec((tm, D), lambda i: (i, 0)),
                  pl.BlockSpec((tm, D), lambda i: (i, 0)),
                  pl.BlockSpec((1, D), lambda i: (0, 0))],
        out_specs=pl.BlockSpec((tm, D), lambda i: (i, 0)),
        out_shape=jax.ShapeDtypeStruct((T, D), out_dtype),
        compiler_params=_params("parallel"),
        name="add_rmsnorm_rows",
    )(x2d, y2d, gain.reshape(1, D).astype(F32))


def _mm_body(a_ref, b_ref, o_ref, *, scale):
    acc = jnp.dot(a_ref[...], b_ref[...], preferred_element_type=F32)
    if scale != 1.0:
        acc = acc * scale
    o_ref[...] = acc.astype(o_ref.dtype)


def _mm_res_body(a_ref, b_ref, r_ref, o_ref):
    acc = jnp.dot(a_ref[...], b_ref[...], preferred_element_type=F32)
    o_ref[...] = (r_ref[...].astype(F32) + acc).astype(o_ref.dtype)


def matmul(a, b, out_dtype, *, residual=None, scale=1.0, tm=1024, tn=512, name="matmul"):
    M = a.shape[0]
    K, N = b.shape
    tm = _tile(M, tm)
    tn = _tile(N, tn)
    in_specs = [pl.BlockSpec((tm, K), lambda i, j: (i, 0)),
                pl.BlockSpec((K, tn), lambda i, j: (0, j))]
    args = [a, b]
    if residual is None:
        body = functools.partial(_mm_body, scale=scale)
    else:
        body = _mm_res_body
        in_specs.append(pl.BlockSpec((tm, tn), lambda i, j: (i, j)))
        args.append(residual)
    return pl.pallas_call(
        body,
        grid=(M // tm, N // tn),
        in_specs=in_specs,
        out_specs=pl.BlockSpec((tm, tn), lambda i, j: (i, j)),
        out_shape=jax.ShapeDtypeStruct((M, N), out_dtype),
        compiler_params=_params("parallel", "arbitrary"),
        name=name,
    )(*args)


def _qk_prep_body(x_ref, g_ref, cos_ref, sin_ref, o_ref, *, heads):
    cos = cos_ref[...]
    sin = sin_ref[...]
    lane = lax.broadcasted_iota(jnp.int32, cos.shape, 1)
    first_half = (lane // (HEAD_DIM // 4)) % 2 == 0
    for hh in range(heads):
        sl = slice(hh * HEAD_DIM, (hh + 1) * HEAD_DIM)
        x = x_ref[:, sl].astype(F32)
        ms = jnp.mean(x * x, axis=-1, keepdims=True)
        y = x * lax.rsqrt(ms + EPS) * g_ref[:, sl]
        up = pltpu.roll(y, HEAD_DIM - HEAD_DIM // 4, 1)
        down = pltpu.roll(y, HEAD_DIM // 4, 1)
        partner = jnp.where(first_half, up, down)
        o_ref[:, sl] = (y * cos + partner * sin).astype(o_ref.dtype)


def qk_prep(proj, col0, gains, cos, sin, seq_len, tm=512):
    T = proj.shape[0]
    width = gains.shape[1]
    tm = _tile(seq_len, tm)
    heads = next(n for n in (4, 2, 1) if col0 % (n * HEAD_DIM) == 0 and width % (n * HEAD_DIM) == 0)
    bw = heads * HEAD_DIM
    cb0 = col0 // bw
    spb = seq_len // tm
    return pl.pallas_call(
        functools.partial(_qk_prep_body, heads=heads),
        grid=(T // tm, width // bw),
        in_specs=[pl.BlockSpec((tm, bw), lambda i, j: (i, cb0 + j)),
                  pl.BlockSpec((1, bw), lambda i, j: (0, j)),
                  pl.BlockSpec((tm, HEAD_DIM), lambda i, j: (i % spb, 0)),
                  pl.BlockSpec((tm, HEAD_DIM), lambda i, j: (i % spb, 0))],
        out_specs=pl.BlockSpec((tm, bw), lambda i, j: (i, j)),
        out_shape=jax.ShapeDtypeStruct((T, width), BF16),
        compiler_params=_params("parallel", "arbitrary"),
        name="qk_prep",
    )(proj, gains, cos, sin)


def rope_tables(seq_len):
    half = HEAD_DIM // 2
    t = jnp.arange(seq_len)
    pos = jnp.stack([t // GRID_W, t % GRID_W], axis=-1).astype(F32)
    inv_freq = ROPE_THETA ** (-jnp.arange(0, half, 2, dtype=F32) / half)
    ang = pos[:, :, None] * inv_freq
    cos = jnp.cos(ang)
    sin = jnp.sin(ang)
    cos_full = jnp.stack([cos, cos], axis=2).reshape(seq_len, HEAD_DIM)
    sin_full = jnp.stack([-sin, sin], axis=2).reshape(seq_len, HEAD_DIM)
    return cos_full, sin_full


def _attn_body(q_ref, k_ref, v_ref, o_ref):
    k = k_ref[...]
    v = v_ref[...]
    for g in range(GQA_GROUP):
        sl = slice(g * HEAD_DIM, (g + 1) * HEAD_DIM)
        s = lax.dot_general(q_ref[:, sl], k, (((1,), (1,)), ((), ())),
                            preferred_element_type=F32)
        m = jnp.max(s, axis=-1, keepdims=True)
        p = jnp.exp(s - m)
        l = jnp.sum(p, axis=-1, keepdims=True)
        o = jnp.dot(p.astype(BF16), v, preferred_element_type=F32)
        o_ref[:, sl] = (o / l).astype(o_ref.dtype)


def gqa_attention(qk, proj, v_col0, B, S, tq=256):
    tq = _tile(S, tq)
    qk3 = qk.reshape(B, S, qk.shape[1])
    proj3 = proj.reshape(B, S, proj.shape[1])
    gw = GQA_GROUP * HEAD_DIM
    kb0 = (N_HEADS * HEAD_DIM) // HEAD_DIM
    vb0 = v_col0 // HEAD_DIM
    out = pl.pallas_call(
        _attn_body,
        grid=(B, N_KV_HEADS, S // tq),
        in_specs=[pl.BlockSpec((None, tq, gw), lambda b, h, i: (b, i, h)),
                  pl.BlockSpec((None, S, HEAD_DIM), lambda b, h, i: (b, 0, kb0 + h)),
                  pl.BlockSpec((None, S, HEAD_DIM), lambda b, h, i: (b, 0, vb0 + h))],
        out_specs=pl.BlockSpec((None, tq, gw), lambda b, h, i: (b, i, h)),
        out_shape=jax.ShapeDtypeStruct((B, S, N_HEADS * HEAD_DIM), BF16),
        compiler_params=_params("parallel", "parallel", "arbitrary"),
        name="gqa_attention",
    )(qk3, qk3, proj3)
    return out.reshape(B * S, N_HEADS * HEAD_DIM)


def dft_tables(n):
    idx = (jnp.arange(n, dtype=jnp.int32)[:, None] * jnp.arange(n, dtype=jnp.int32)[None, :]) % n
    ang = idx.astype(F32) * (2.0 * math.pi / n)
    return jnp.cos(ang), jnp.sin(ang)


def _dft2_body(cs_ref, ss_ref, pc_ref, ps_ref, o_ref, *, scale):
    acc = jnp.dot(cs_ref[...], pc_ref[...], preferred_element_type=F32)
    acc = acc - jnp.dot(ss_ref[...], ps_ref[...], preferred_element_type=F32)
    o_ref[...] = (acc * scale).astype(o_ref.dtype)


def dft_positions(cs, ss, p, B, S, C, scale, tm=512, tn=512):
    tm = _tile(S, tm)
    tn = _tile(C, tn)
    p3 = p.reshape(B, S, 2 * C)
    nj = C // tn
    out = pl.pallas_call(
        functools.partial(_dft2_body, scale=scale),
        grid=(S // tm, B, nj),
        in_specs=[pl.BlockSpec((tm, S), lambda i, b, j: (i, 0)),
                  pl.BlockSpec((tm, S), lambda i, b, j: (i, 0)),
                  pl.BlockSpec((None, S, tn), lambda i, b, j: (b, 0, j)),
                  pl.BlockSpec((None, S, tn), lambda i, b, j: (b, 0, nj + j))],
        out_specs=pl.BlockSpec((None, tm, tn), lambda i, b, j: (b, i, j)),
        out_shape=jax.ShapeDtypeStruct((B, S, C), BF16),
        compiler_params=_params("parallel", "arbitrary", "arbitrary"),
        name="dft_positions",
    )(cs, ss, p3, p3)
    return out.reshape(B * S, C)


def _merge_body(o_ref, f_ref, h_ref, wa_ref, wf_ref, wg0_ref, wg1_ref, b0_ref, b1_ref, out_ref):
    h = h_ref[...]
    a_br = jnp.dot(o_ref[...], wa_ref[...], preferred_element_type=F32)
    f_br = jnp.dot(f_ref[...], wf_ref[...], preferred_element_type=F32)
    g0 = jax.nn.sigmoid(jnp.dot(h, wg0_ref[...], preferred_element_type=F32) + b0_ref[...])
    g1 = jax.nn.sigmoid(jnp.dot(h, wg1_ref[...], preferred_element_type=F32) + b1_ref[...])
    out_ref[...] = (g0 * a_br + g1 * f_br).astype(out_ref.dtype)


def branch_merge(o, fm, h, wa, wf, wg, bg, tm=512, tn=512):
    T, D = h.shape
    tm = _tile(T, tm)
    tn = _tile(D, tn)
    nj = D // tn
    ka = o.shape[1]
    kf = fm.shape[1]
    bg2 = bg.reshape(1, 2 * D).astype(F32)
    return pl.pallas_call(
        _merge_body,
        grid=(T // tm, nj),
        in_specs=[pl.BlockSpec((tm, ka), lambda i, j: (i, 0)),
                  pl.BlockSpec((tm, kf), lambda i, j: (i, 0)),
                  pl.BlockSpec((tm, D), lambda i, j: (i, 0)),
                  pl.BlockSpec((ka, tn), lambda i, j: (0, j)),
                  pl.BlockSpec((kf, tn), lambda i, j: (0, j)),
                  pl.BlockSpec((D, tn), lambda i, j: (0, j)),
                  pl.BlockSpec((D, tn), lambda i, j: (0, nj + j)),
                  pl.BlockSpec((1, tn), lambda i, j: (0, j)),
                  pl.BlockSpec((1, tn), lambda i, j: (0, nj + j))],
        out_specs=pl.BlockSpec((tm, tn), lambda i, j: (i, j)),
        out_shape=jax.ShapeDtypeStruct((T, D), BF16),
        compiler_params=_params("parallel", "arbitrary"),
        name="branch_merge",
    )(o, fm, h, wa, wf, wg, wg, bg2, bg2)


def _cross_attn_body(q_ref, kv_ref, o_ref):
    w = CA_HEADS * CA_HEAD_DIM
    for hh in range(CA_HEADS):
        sl = slice(hh * CA_HEAD_DIM, (hh + 1) * CA_HEAD_DIM)
        k = kv_ref[:, sl]
        v = kv_ref[:, w + hh * CA_HEAD_DIM: w + (hh + 1) * CA_HEAD_DIM]
        s = lax.dot_general(q_ref[:, sl], k, (((1,), (1,)), ((), ())),
                            preferred_element_type=F32)
        m = jnp.max(s, axis=-1, keepdims=True)
        p = jnp.exp(s - m)
        l = jnp.sum(p, axis=-1, keepdims=True)
        o = jnp.dot(p.astype(BF16), v, preferred_element_type=F32)
        o_ref[:, sl] = (o / l).astype(o_ref.dtype)


def cross_attention(qc, kv, B, S, M, tq=512):
    tq = _tile(S, tq)
    w = CA_HEADS * CA_HEAD_DIM
    out = pl.pallas_call(
        _cross_attn_body,
        grid=(B, S // tq),
        in_specs=[pl.BlockSpec((None, tq, w), lambda b, i: (b, i, 0)),
                  pl.BlockSpec((None, M, 2 * w), lambda b, i: (b, 0, 0))],
        out_specs=pl.BlockSpec((None, tq, w), lambda b, i: (b, i, 0)),
        out_shape=jax.ShapeDtypeStruct((B, S, w), BF16),
        compiler_params=_params("parallel", "arbitrary"),
        name="cross_attention",
    )(qc.reshape(B, S, w), kv.reshape(B, M, 2 * w))
    return out.reshape(B * S, w)


def _topk_desc(s, k):
    vals = []
    for _ in range(k):
        m = jnp.max(s, axis=0, keepdims=True)
        vals.append(m)
        s = jnp.where(s == m, -jnp.inf, s)
    return vals


def _peer_route_body(h_ref, wq_ref, keys_ref, s1_ref, s2_ref, e1_ref, e2_ref, tau_ref, q_scr, *, tl):
    q_scr[...] = lax.dot_general(wq_ref[...], h_ref[...], (((1,), (1,)), ((), ())),
                                 preferred_element_type=F32)
    tm = h_ref.shape[0]
    k1 = keys_ref[0]
    k2 = keys_ref[1]

    def per_head(hh, carry):
        r0 = pl.multiple_of(hh * (2 * PEER_HALF), 2 * PEER_HALF)
        for lc in range(tm // tl):
            ls = slice(lc * tl, (lc + 1) * tl)
            s1 = jnp.dot(k1, q_scr[pl.ds(r0, PEER_HALF), ls], preferred_element_type=F32,
                         precision=lax.Precision.HIGHEST)
            s2 = jnp.dot(k2, q_scr[pl.ds(r0 + PEER_HALF, PEER_HALF), ls],
                         preferred_element_type=F32, precision=lax.Precision.HIGHEST)
            t1 = _topk_desc(s1, PEER_TOPK)
            t2 = _topk_desc(s2, PEER_TOPK)
            t2m = jnp.concatenate(t2, axis=0)
            cand = jnp.concatenate([t1[a] + t2m for a in range(PEER_TOPK)], axis=0)
            tau = _topk_desc(cand, PEER_TOPK)[-1]
            m1 = t1[0]
            m2 = t2[0]
            z = jnp.sum(jnp.where(cand >= tau, jnp.exp(cand - (m1 + m2)), 0.0),
                        axis=0, keepdims=True)
            s1_ref[hh, :, ls] = s1
            s2_ref[hh, :, ls] = s2
            e1_ref[hh, :, ls] = jnp.exp(s1 - m1)
            e2_ref[hh, :, ls] = jnp.exp(s2 - m2) / z
            tau_ref[hh, :, ls] = tau
        return carry

    lax.fori_loop(0, PEER_HEADS, per_head, 0)


def peer_route(hf, wq_t, keys, tm=256, tl=128):
    T, D = hf.shape
    tm = _tile(T, tm)
    tl = _tile(tm, tl)
    qw = wq_t.shape[0]
    big = jax.ShapeDtypeStruct((PEER_HEADS, N_KEYS, T), F32)
    big_spec = pl.BlockSpec((PEER_HEADS, N_KEYS, tm), lambda i: (0, 0, i))
    return pl.pallas_call(
        functools.partial(_peer_route_body, tl=tl),
        grid=(T // tm,),
        in_specs=[pl.BlockSpec((tm, D), lambda i: (i, 0)),
                  pl.BlockSpec((qw, D), lambda i: (0, 0)),
                  pl.BlockSpec((2, N_KEYS, PEER_HALF), lambda i: (0, 0, 0))],
        out_specs=[big_spec, big_spec, big_spec, big_spec,
                   pl.BlockSpec((PEER_HEADS, 1, tm), lambda i: (0, 0, i))],
        out_shape=[big, big, big, big, jax.ShapeDtypeStruct((PEER_HEADS, 1, T), F32)],
        scratch_shapes=[pltpu.VMEM((qw, tm), F32)],
        compiler_params=_params("parallel"),
        name="peer_route",
    )(hf, wq_t, keys)


def _peer_expert_body(h_ref, u_ref, v_ref, s1_ref, s2_ref, e1_ref, e2_ref, tau_ref, o_ref, w_scr, *, nc):
    k = pl.program_id(1)

    @pl.when(k == 0)
    def _():
        o_ref[...] = jnp.zeros_like(o_ref)

    a_t = lax.dot_general(u_ref[...], h_ref[...], (((1,), (1,)), ((), ())),
                          preferred_element_type=F32)
    act = 0.5 * a_t * (1.0 + lax.erf(a_t * INV_SQRT2))
    for cc in range(nc):
        c = k * nc + cc
        g = None
        for hh in range(PEER_HEADS):
            s1c = s1_ref[hh, pl.ds(c, 1), :]
            e1c = e1_ref[hh, pl.ds(c, 1), :]
            mask = (s2_ref[hh] + s1c) >= tau_ref[hh]
            term = jnp.where(mask, e2_ref[hh] * e1c, 0.0)
            g = term if g is None else g + term
        rows = slice(cc * N_KEYS, (cc + 1) * N_KEYS)
        w_scr[rows, :] = (act[rows, :] * g).astype(BF16)
    o_ref[...] += lax.dot_general(w_scr[...], v_ref[...], (((0,), (0,)), ((), ())),
                                  preferred_element_type=F32)


def peer_experts(hf, u, v, s1, s2, e1, e2, tau, tm=512, ec=256):
    T, D = hf.shape
    NE = u.shape[0]
    tm = _tile(T, tm)
    ec = _tile(NE, ec)
    nc = ec // N_KEYS
    once = pl.Buffered(1)
    big_spec = pl.BlockSpec((PEER_HEADS, N_KEYS, tm), lambda i, k: (0, 0, i), pipeline_mode=once)
    return pl.pallas_call(
        functools.partial(_peer_expert_body, nc=nc),
        grid=(T // tm, NE // ec),
        in_specs=[pl.BlockSpec((tm, D), lambda i, k: (i, 0), pipeline_mode=once),
                  pl.BlockSpec((ec, D), lambda i, k: (k, 0)),
                  pl.BlockSpec((ec, D), lambda i, k: (k, 0)),
                  big_spec, big_spec, big_spec, big_spec,
                  pl.BlockSpec((PEER_HEADS, 1, tm), lambda i, k: (0, 0, i), pipeline_mode=once)],
        out_specs=pl.BlockSpec((tm, D), lambda i, k: (i, 0)),
        out_shape=jax.ShapeDtypeStruct((T, D), F32),
        scratch_shapes=[pltpu.VMEM((ec, tm), BF16)],
        compiler_params=_params("parallel", "arbitrary"),
        name="peer_experts",
    )(hf, u, v, s1, s2, e1, e2, tau)


def _trunk(x, mem, w):
    B, S, D = x.shape
    M = mem.shape[1]
    T = B * S
    C = w["four_cols"]
    x2d = x.reshape(T, D)

    h = rmsnorm_rows(x2d, w["norm_mix"], BF16)
    proj = matmul(h, w["w_in"], BF16, name="in_proj")

    cos, sin = rope_tables(S)
    qk = qk_prep(proj, C, w["qk_gain"], cos, sin, S)
    o = gqa_attention(qk, proj, C + (N_HEADS + N_KV_HEADS) * HEAD_DIM, B, S)

    cs, ss = dft_tables(S)
    p = matmul(proj, w["dft_ch"], BF16, name="dft_channels")
    fm = dft_positions(cs.astype(BF16), ss.astype(BF16), p, B, S, C, 1.0 / math.sqrt(S * C))

    merged = branch_merge(o, fm, h, w["w_attn_br"], w["w_four_br"], w["w_gate"], w["b_gate"])
    x1 = matmul(merged, w["w_out"], F32, residual=x2d, name="out_proj")

    hc = rmsnorm_rows(x1, w["norm_ca"], BF16)
    qc = matmul(hc, w["w_cq"], BF16, scale=CA_HEAD_DIM ** -0.5, name="ca_q")
    mn = rmsnorm_rows(mem.reshape(B * M, D), w["mem_norm"], BF16)
    kv = matmul(mn, w["w_ckv"], BF16, name="ca_kv")
    oc = cross_attention(qc, kv, B, S, M)
    x2 = matmul(oc, w["w_co"], F32, residual=x1, name="ca_out")

    hf = rmsnorm_rows(x2, w["norm_ffn"], BF16)
    s1, s2, e1, e2, tau = peer_route(hf, w["w_pq_t"], w["sub_keys"])
    po = peer_experts(hf, w["expert_u"], w["expert_v"], s1, s2, e1, e2, tau)
    y = add_rmsnorm_rows(x2, po, w["final_norm"], F32)
    return y.reshape(B, S, D)


def kernel(x_prompt, x_sample, mem_prompt, mem_sample, norm_mix, w_in, q_norm, k_norm, w_attn_br, w_four_br, w_gate, b_gate, w_out, norm_ca, mem_norm, w_cq, w_ckv, w_co, norm_ffn, w_pq, sub_keys, expert_u, expert_v, final_norm):
    assert norm_mix.shape[0] == 1, "single-layer trunk"
    C = w_four_br.shape[1]
    attn_w = N_HEADS * HEAD_DIM
    kv_w = N_KV_HEADS * HEAD_DIM
    wi = w_in[0]
    w_in_r = jnp.concatenate([wi[:, attn_w + 2 * kv_w:], wi[:, :attn_w + 2 * kv_w]], axis=1).astype(BF16)
    cc, sc = dft_tables(C)
    dft_ch = jnp.concatenate([cc, sc], axis=1).astype(BF16)
    scale = HEAD_DIM ** -0.5
    qk_gain = jnp.concatenate([jnp.tile(q_norm[0] * scale, N_HEADS),
                               jnp.tile(k_norm[0], N_KV_HEADS)]).reshape(1, -1).astype(F32)
    w = dict(
        four_cols=C,
        norm_mix=norm_mix[0], w_in=w_in_r, qk_gain=qk_gain, dft_ch=dft_ch,
        w_attn_br=w_attn_br[0].astype(BF16), w_four_br=w_four_br[0].astype(BF16),
        w_gate=w_gate[0].astype(BF16), b_gate=b_gate[0], w_out=w_out[0].astype(BF16),
        norm_ca=norm_ca[0], mem_norm=mem_norm[0], w_cq=w_cq[0].astype(BF16),
        w_ckv=w_ckv[0].astype(BF16), w_co=w_co[0].astype(BF16), norm_ffn=norm_ffn[0],
        w_pq_t=w_pq[0].T.astype(BF16), sub_keys=sub_keys[0].astype(F32),
        expert_u=expert_u[0].astype(BF16), expert_v=expert_v[0].astype(BF16),
        final_norm=final_norm,
    )
    y_prompt = _trunk(x_prompt, mem_prompt, w)
    y_sample = _trunk(x_sample, mem_sample, w)
    return (y_prompt, y_sample)
```

```python
import functools
import math

import jax
import jax.numpy as jnp
from jax import lax
from jax.experimental import pallas as pl
from jax.experimental.pallas import tpu as pltpu

F32 = jnp.float32
BF16 = jnp.bfloat16

N_HEADS = 16
N_KV_HEADS = 4
HEAD_DIM = 128
GQA_GROUP = N_HEADS // N_KV_HEADS
ROPE_THETA = 10000.0
GRID_W = 64
CA_HEADS = 4
CA_HEAD_DIM = 256
N_KEYS = 128
PEER_HEADS = 8
PEER_TOPK = 16
PEER_HALF = 128
EPS = 1e-6
INV_SQRT2 = 0.7071067811865476

V7X_VMEM_BYTES = 64 * 1024 * 1024
VMEM_LIMIT = V7X_VMEM_BYTES - 8 * 1024 * 1024
LANES = 128


def _tile(dim, pref):
    t = min(pref, dim)
    while dim % t:
        t //= 2
    return t


def _params(*sem):
    return pltpu.CompilerParams(dimension_semantics=sem, vmem_limit_bytes=VMEM_LIMIT)


def _rmsnorm_body(x_ref, g_ref, o_ref):
    x = x_ref[...].astype(F32)
    ms = jnp.mean(x * x, axis=-1, keepdims=True)
    o_ref[...] = (x * lax.rsqrt(ms + EPS) * g_ref[...]).astype(o_ref.dtype)


def rmsnorm_rows(x2d, gain, out_dtype, tm=256):
    T, D = x2d.shape
    tm = _tile(T, tm)
    return pl.pallas_call(
        _rmsnorm_body,
        grid=(T // tm,),
        in_specs=[pl.BlockSpec((tm, D), lambda i: (i, 0)),
                  pl.BlockSpec((1, D), lambda i: (0, 0))],
        out_specs=pl.BlockSpec((tm, D), lambda i: (i, 0)),
        out_shape=jax.ShapeDtypeStruct((T, D), out_dtype),
        compiler_params=_params("parallel"),
        name="rmsnorm_rows",
    )(x2d, gain.reshape(1, D).astype(F32))


def _add_rmsnorm_body(x_ref, y_ref, g_ref, o_ref):
    x = x_ref[...].astype(F32) + y_ref[...].astype(F32)
    ms = jnp.mean(x * x, axis=-1, keepdims=True)
    o_ref[...] = (x * lax.rsqrt(ms + EPS) * g_ref[...]).astype(o_ref.dtype)


def add_rmsnorm_rows(x2d, y2d, gain, out_dtype, tm=256):
    T, D = x2d.shape
    tm = _tile(T, tm)
    return pl.pallas_call(
        _add_rmsnorm_body,
        grid=(T // tm,),
        in_specs=[pl.BlockSpec((tm, D), lambda i: (i, 0)),
                  pl.BlockSpec((tm, D), lambda i: (i, 0)),
                  pl.BlockSpec((1, D), lambda i: (0, 0))],
        out_specs=pl.BlockSpec((tm, D), lambda i: (i, 0)),
        out_shape=jax.ShapeDtypeStruct((T, D), out_dtype),
        compiler_params=_params("parallel"),
        name="add_rmsnorm_rows",
    )(x2d, y2d, gain.reshape(1, D).astype(F32))


def _mm_body(a_ref, b_ref, o_ref, *, scale):
    acc = jnp.dot(a_ref[...], b_ref[...], preferred_element_type=F32)
    if scale != 1.0:
        acc = acc * scale
    o_ref[...] = acc.astype(o_ref.dtype)


def _mm_res_body(a_ref, b_ref, r_ref, o_ref):
    acc = jnp.dot(a_ref[...], b_ref[...], preferred_element_type=F32)
    o_ref[...] = (r_ref[...].astype(F32) + acc).astype(o_ref.dtype)


def matmul(a, b, out_dtype, *, residual=None, scale=1.0, tm=1024, tn=512, name="matmul"):
    M = a.shape[0]
    K, N = b.shape
    tm = _tile(M, tm)
    tn = _tile(N, tn)
    in_specs = [pl.BlockSpec((tm, K), lambda i, j: (i, 0)),
                pl.BlockSpec((K, tn), lambda i, j: (0, j))]
    args = [a, b]
    if residual is None:
        body = functools.partial(_mm_body, scale=scale)
    else:
        body = _mm_res_body
        in_specs.append(pl.BlockSpec((tm, tn), lambda i, j: (i, j)))
        args.append(residual)
    return pl.pallas_call(
        body,
        grid=(M // tm, N // tn),
        in_specs=in_specs,
        out_specs=pl.BlockSpec((tm, tn), lambda i, j: (i, j)),
        out_shape=jax.ShapeDtypeStruct((M, N), out_dtype),
        compiler_params=_params("parallel", "arbitrary"),
        name=name,
    )(*args)


def _qk_prep_body(x_ref, g_ref, cos_ref, sin_ref, o_ref, *, heads):
    cos = cos_ref[...]
    sin = sin_ref[...]
    lane = lax.broadcasted_iota(jnp.int32, cos.shape, 1)
    first_half = (lane // (HEAD_DIM // 4)) % 2 == 0
    for hh in range(heads):
        sl = slice(hh * HEAD_DIM, (hh + 1) * HEAD_DIM)
        x = x_ref[:, sl].astype(F32)
        ms = jnp.mean(x * x, axis=-1, keepdims=True)
        y = x * lax.rsqrt(ms + EPS) * g_ref[:, sl]
        up = pltpu.roll(y, HEAD_DIM - HEAD_DIM // 4, 1)
        down = pltpu.roll(y, HEAD_DIM // 4, 1)
        partner = jnp.where(first_half, up, down)
        o_ref[:, sl] = (y * cos + partner * sin).astype(o_ref.dtype)


def qk_prep(proj, col0, gains, cos, sin, seq_len, tm=512):
    T = proj.shape[0]
    width = gains.shape[1]
    tm = _tile(seq_len, tm)
    heads = next(n for n in (4, 2, 1) if col0 % (n * HEAD_DIM) == 0 and width % (n * HEAD_DIM) == 0)
    bw = heads * HEAD_DIM
    cb0 = col0 // bw
    spb = seq_len // tm
    return pl.pallas_call(
        functools.partial(_qk_prep_body, heads=heads),
        grid=(T // tm, width // bw),
        in_specs=[pl.BlockSpec((tm, bw), lambda i, j: (i, cb0 + j)),
                  pl.BlockSpec((1, bw), lambda i, j: (0, j)),
                  pl.BlockSpec((tm, HEAD_DIM), lambda i, j: (i % spb, 0)),
                  pl.BlockSpec((tm, HEAD_DIM), lambda i, j: (i % spb, 0))],
        out_specs=pl.BlockSpec((tm, bw), lambda i, j: (i, j)),
        out_shape=jax.ShapeDtypeStruct((T, width), BF16),
        compiler_params=_params("parallel", "arbitrary"),
        name="qk_prep",
    )(proj, gains, cos, sin)


def rope_tables(seq_len):
    half = HEAD_DIM // 2
    t = jnp.arange(seq_len)
    pos = jnp.stack([t // GRID_W, t % GRID_W], axis=-1).astype(F32)
    inv_freq = ROPE_THETA ** (-jnp.arange(0, half, 2, dtype=F32) / half)
    ang = pos[:, :, None] * inv_freq
    cos = jnp.cos(ang)
    sin = jnp.sin(ang)
    cos_full = jnp.stack([cos, cos], axis=2).reshape(seq_len, HEAD_DIM)
    sin_full = jnp.stack([-sin, sin], axis=2).reshape(seq_len, HEAD_DIM)
    return cos_full, sin_full


def _attn_body(q_ref, k_ref, v_ref, o_ref):
    k = k_ref[...]
    v = v_ref[...]
    for g in range(GQA_GROUP):
        sl = slice(g * HEAD_DIM, (g + 1) * HEAD_DIM)
        s = lax.dot_general(q_ref[:, sl], k, (((1,), (1,)), ((), ())),
                            preferred_element_type=F32)
        m = jnp.max(s, axis=-1, keepdims=True)
        p = jnp.exp(s - m)
        l = jnp.sum(p, axis=-1, keepdims=True)
        o = jnp.dot(p.astype(BF16), v, preferred_element_type=F32)
        o_ref[:, sl] = (o / l).astype(o_ref.dtype)


def gqa_attention(qk, proj, v_col0, B, S, tq=256):
    tq = _tile(S, tq)
    qk3 = qk.reshape(B, S, qk.shape[1])
    proj3 = proj.reshape(B, S, proj.shape[1])
    gw = GQA_GROUP * HEAD_DIM
    kb0 = (N_HEADS * HEAD_DIM) // HEAD_DIM
    vb0 = v_col0 // HEAD_DIM
    out = pl.pallas_call(
        _attn_body,
        grid=(B, N_KV_HEADS, S // tq),
        in_specs=[pl.BlockSpec((None, tq, gw), lambda b, h, i: (b, i, h)),
                  pl.BlockSpec((None, S, HEAD_DIM), lambda b, h, i: (b, 0, kb0 + h)),
                  pl.BlockSpec((None, S, HEAD_DIM), lambda b, h, i: (b, 0, vb0 + h))],
        out_specs=pl.BlockSpec((None, tq, gw), lambda b, h, i: (b, i, h)),
        out_shape=jax.ShapeDtypeStruct((B, S, N_HEADS * HEAD_DIM), BF16),
        compiler_params=_params("parallel", "parallel", "arbitrary"),
        name="gqa_attention",
    )(qk3, qk3, proj3)
    return out.reshape(B * S, N_HEADS * HEAD_DIM)


def dft_tables(n, dtype=BF16):
    r = 64 if n % 64 == 0 else 1
    k = jnp.arange(n, dtype=jnp.int32)[None, :]

    def thin(rows):
        ang = ((rows[:, None] * k) % n).astype(F32) * (2.0 * math.pi / n)
        return jnp.cos(ang), jnp.sin(ang)

    c_hi, s_hi = thin(jnp.arange(n // r, dtype=jnp.int32) * r)
    c_lo, s_lo = thin(jnp.arange(r, dtype=jnp.int32))
    cos = c_hi[:, None, :] * c_lo[None, :, :] - s_hi[:, None, :] * s_lo[None, :, :]
    sin = s_hi[:, None, :] * c_lo[None, :, :] + c_hi[:, None, :] * s_lo[None, :, :]
    return cos.reshape(n, n).astype(dtype), sin.reshape(n, n).astype(dtype)


def _dft2_body(cs_ref, ss_ref, pc_ref, ps_ref, o_ref, *, scale):
    acc = jnp.dot(cs_ref[...], pc_ref[...], preferred_element_type=F32)
    acc = acc - jnp.dot(ss_ref[...], ps_ref[...], preferred_element_type=F32)
    o_ref[...] = (acc * scale).astype(o_ref.dtype)


def dft_positions(cs, ss, p, B, S, C, scale, tm=512, tn=512):
    tm = _tile(S, tm)
    tn = _tile(C, tn)
    p3 = p.reshape(B, S, 2 * C)
    nj = C // tn
    out = pl.pallas_call(
        functools.partial(_dft2_body, scale=scale),
        grid=(S // tm, B, nj),
        in_specs=[pl.BlockSpec((tm, S), lambda i, b, j: (i, 0)),
                  pl.BlockSpec((tm, S), lambda i, b, j: (i, 0)),
                  pl.BlockSpec((None, S, tn), lambda i, b, j: (b, 0, j)),
                  pl.BlockSpec((None, S, tn), lambda i, b, j: (b, 0, nj + j))],
        out_specs=pl.BlockSpec((None, tm, tn), lambda i, b, j: (b, i, j)),
        out_shape=jax.ShapeDtypeStruct((B, S, C), BF16),
        compiler_params=_params("parallel", "arbitrary", "arbitrary"),
        name="dft_positions",
    )(cs, ss, p3, p3)
    return out.reshape(B * S, C)


def _merge_body(o_ref, f_ref, h_ref, wa_ref, wf_ref, wg0_ref, wg1_ref, b0_ref, b1_ref, out_ref):
    h = h_ref[...]
    a_br = jnp.dot(o_ref[...], wa_ref[...], preferred_element_type=F32)
    f_br = jnp.dot(f_ref[...], wf_ref[...], preferred_element_type=F32)
    g0 = jax.nn.sigmoid(jnp.dot(h, wg0_ref[...], preferred_element_type=F32) + b0_ref[...])
    g1 = jax.nn.sigmoid(jnp.dot(h, wg1_ref[...], preferred_element_type=F32) + b1_ref[...])
    out_ref[...] = (g0 * a_br + g1 * f_br).astype(out_ref.dtype)


def branch_merge(o, fm, h, wa, wf, wg, bg, tm=512, tn=512):
    T, D = h.shape
    tm = _tile(T, tm)
    tn = _tile(D, tn)
    nj = D // tn
    ka = o.shape[1]
    kf = fm.shape[1]
    bg2 = bg.reshape(1, 2 * D).astype(F32)
    return pl.pallas_call(
        _merge_body,
        grid=(T // tm, nj),
        in_specs=[pl.BlockSpec((tm, ka), lambda i, j: (i, 0)),
                  pl.BlockSpec((tm, kf), lambda i, j: (i, 0)),
                  pl.BlockSpec((tm, D), lambda i, j: (i, 0)),
                  pl.BlockSpec((ka, tn), lambda i, j: (0, j)),
                  pl.BlockSpec((kf, tn), lambda i, j: (0, j)),
                  pl.BlockSpec((D, tn), lambda i, j: (0, j)),
                  pl.BlockSpec((D, tn), lambda i, j: (0, nj + j)),
                  pl.BlockSpec((1, tn), lambda i, j: (0, j)),
                  pl.BlockSpec((1, tn), lambda i, j: (0, nj + j))],
        out_specs=pl.BlockSpec((tm, tn), lambda i, j: (i, j)),
        out_shape=jax.ShapeDtypeStruct((T, D), BF16),
        compiler_params=_params("parallel", "arbitrary"),
        name="branch_merge",
    )(o, fm, h, wa, wf, wg, wg, bg2, bg2)


def _cross_attn_body(q_ref, kv_ref, o_ref):
    w = CA_HEADS * CA_HEAD_DIM
    for hh in range(CA_HEADS):
        sl = slice(hh * CA_HEAD_DIM, (hh + 1) * CA_HEAD_DIM)
        k = kv_ref[:, sl]
        v = kv_ref[:, w + hh * CA_HEAD_DIM: w + (hh + 1) * CA_HEAD_DIM]
        s = lax.dot_general(q_ref[:, sl], k, (((1,), (1,)), ((), ())),
                            preferred_element_type=F32)
        m = jnp.max(s, axis=-1, keepdims=True)
        p = jnp.exp(s - m)
        l = jnp.sum(p, axis=-1, keepdims=True)
        o = jnp.dot(p.astype(BF16), v, preferred_element_type=F32)
        o_ref[:, sl] = (o / l).astype(o_ref.dtype)


def cross_attention(qc, kv, B, S, M, tq=512):
    tq = _tile(S, tq)
    w = CA_HEADS * CA_HEAD_DIM
    out = pl.pallas_call(
        _cross_attn_body,
        grid=(B, S // tq),
        in_specs=[pl.BlockSpec((None, tq, w), lambda b, i: (b, i, 0)),
                  pl.BlockSpec((None, M, 2 * w), lambda b, i: (b, 0, 0))],
        out_specs=pl.BlockSpec((None, tq, w), lambda b, i: (b, i, 0)),
        out_shape=jax.ShapeDtypeStruct((B, S, w), BF16),
        compiler_params=_params("parallel", "arbitrary"),
        name="cross_attention",
    )(qc.reshape(B, S, w), kv.reshape(B, M, 2 * w))
    return out.reshape(B * S, w)


SUBLANES = 8


def _sort_network(n):
    pairs = []

    def merge(lo, length, r):
        step = r * 2
        if step < length:
            merge(lo, length, step)
            merge(lo + r, length, step)
            pairs.extend((i, i + r) for i in range(lo + r, lo + length - r, step))
        else:
            pairs.append((lo, lo + r))

    def sort(lo, length):
        if length > 1:
            half = length // 2
            sort(lo, half)
            sort(lo + half, half)
            merge(lo, length, 1)

    sort(0, 16)
    return [(i, j) for i, j in pairs if j < n]


def _compare_exchange(v, i, j):
    v[i], v[j] = jnp.maximum(v[i], v[j]), jnp.minimum(v[i], v[j])


def _bitonic_sort16(v):
    stride = 8
    while stride:
        for i in range(16):
            if not i & stride:
                _compare_exchange(v, i, i + stride)
        stride //= 2


def _merge_sublanes(v, shift, n_valid=16):
    def other(i):
        return pltpu.roll(v[i], shift, 0)

    out = []
    for i in range(16):
        j = 15 - i
        if i < n_valid and j < n_valid:
            out.append(jnp.maximum(v[i], other(j)))
        elif i < n_valid:
            out.append(v[i])
        else:
            out.append(other(j))
    return out


def _top16_sorted(s):
    v = [s[r * SUBLANES:(r + 1) * SUBLANES, :] for r in range(16)]
    for i, j in _sort_network(16):
        _compare_exchange(v, i, j)
    for shift in (4, 2, 1):
        v = _merge_sublanes(v, shift)
        _bitonic_sort16(v)
    return v


def _pack_sublanes(vals):
    row = lax.broadcasted_iota(jnp.int32, vals[0].shape, 0)
    out = vals[SUBLANES - 1]
    for r in range(SUBLANES - 2, -1, -1):
        out = jnp.where(row == r, vals[r], out)
    return out


def _pair_threshold(t1, t2):
    p_lo = _pack_sublanes(t2[:SUBLANES])
    p_hi = _pack_sublanes(t2[SUBLANES:])
    q_hi = _pack_sublanes(t1[SUBLANES:])
    cand = [t1[0] + p_lo, t1[0] + p_hi] + [t1[a] + p_lo for a in range(1, SUBLANES)] + [q_hi + t2[0]]
    v = list(cand)
    n = len(v)
    for i, j in _sort_network(n):
        _compare_exchange(v, i, j)
    v = _merge_sublanes(v, 4, n_valid=n)
    _bitonic_sort16(v)
    v = _merge_sublanes(v, 2)
    _bitonic_sort16(v)
    v = _merge_sublanes(v, 1)
    tau = functools.reduce(jnp.minimum, v)
    return tau, cand


def _peer_route_body(h_ref, wq_ref, keys_ref, s1_ref, s2_ref, e1_ref, e2_ref, tau_ref, q_scr, *, tl):
    q_scr[...] = lax.dot_general(wq_ref[...], h_ref[...], (((1,), (1,)), ((), ())),
                                 preferred_element_type=F32)
    tm = h_ref.shape[0]
    k1 = keys_ref[0]
    k2 = keys_ref[1]

    def per_head(hh, carry):
        r0 = pl.multiple_of(hh * (2 * PEER_HALF), 2 * PEER_HALF)
        for lc in range(tm // tl):
            ls = slice(lc * tl, (lc + 1) * tl)
            s1 = jnp.dot(k1, q_scr[pl.ds(r0, PEER_HALF), ls], preferred_element_type=F32,
                         precision=lax.Precision.HIGHEST)
            s2 = jnp.dot(k2, q_scr[pl.ds(r0 + PEER_HALF, PEER_HALF), ls],
                         preferred_element_type=F32, precision=lax.Precision.HIGHEST)
            t1 = _top16_sorted(s1)
            t2 = _top16_sorted(s2)
            tau, cand = _pair_threshold(t1, t2)
            m1 = t1[0][:1]
            m2 = t2[0][:1]
            top = t1[0] + t2[0]
            zs = [jnp.where(c >= tau, jnp.exp(c - top), 0.0) for c in cand]
            z = jnp.sum(functools.reduce(lambda x, y: x + y, zs), axis=0, keepdims=True)
            s1_ref[hh, :, ls] = s1
            s2_ref[hh, :, ls] = s2
            e1_ref[hh, :, ls] = jnp.exp(s1 - m1)
            e2_ref[hh, :, ls] = jnp.exp(s2 - m2) / z
            tau_ref[hh, :, ls] = tau[:1]
        return carry

    lax.fori_loop(0, PEER_HEADS, per_head, 0)


def peer_route(hf, wq_t, keys, tm=256, tl=128):
    T, D = hf.shape
    tm = _tile(T, tm)
    tl = _tile(tm, tl)
    qw = wq_t.shape[0]
    big = jax.ShapeDtypeStruct((PEER_HEADS, N_KEYS, T), F32)
    big_spec = pl.BlockSpec((PEER_HEADS, N_KEYS, tm), lambda i: (0, 0, i))
    return pl.pallas_call(
        functools.partial(_peer_route_body, tl=tl),
        grid=(T // tm,),
        in_specs=[pl.BlockSpec((tm, D), lambda i: (i, 0)),
                  pl.BlockSpec((qw, D), lambda i: (0, 0)),
                  pl.BlockSpec((2, N_KEYS, PEER_HALF), lambda i: (0, 0, 0))],
        out_specs=[big_spec, big_spec, big_spec, big_spec,
                   pl.BlockSpec((PEER_HEADS, 1, tm), lambda i: (0, 0, i))],
        out_shape=[big, big, big, big, jax.ShapeDtypeStruct((PEER_HEADS, 1, T), F32)],
        scratch_shapes=[pltpu.VMEM((qw, tm), F32)],
        compiler_params=_params("parallel"),
        name="peer_route",
    )(hf, wq_t, keys)


def _peer_expert_body(h_ref, u_ref, v_ref, s1_ref, s2_ref, e1_ref, e2_ref, tau_ref, o_ref,
                      a0, a1, w0, w1, *, nc, nk, n_chunks):
    s = pl.program_id(0)

    @pl.when(s == 0)
    def _():
        for ref in (a0, a1, w0, w1):
            ref[...] = jnp.zeros_like(ref)

    @pl.when((s == 0) | ((s >= 2) & ((s - 2) % nk == 0)))
    def _():
        o_ref[...] = jnp.zeros_like(o_ref)

    ec, tm = a0.shape
    d_model = o_ref.shape[1]
    chunk = jnp.clip(s - 1, 0, n_chunks - 1) % nk
    tok_w = min(tm, 2 * LANES)
    dcol_w = min(d_model, 2 * LANES)

    def project_piece(a_wr, tc):
        cols = slice(tc * tok_w, (tc + 1) * tok_w)
        a_wr[:, cols] = lax.dot_general(u_ref[...], h_ref[cols, :], (((1,), (1,)), ((), ())),
                                        preferred_element_type=F32)

    def gate_piece(a_rd, w_wr, cc, tl):
        c = chunk * nc + cc
        rows = slice(cc * N_KEYS, (cc + 1) * N_KEYS)
        cols = slice(tl * LANES, (tl + 1) * LANES)
        g = None
        for hh in range(PEER_HEADS):
            s1c = s1_ref[hh, pl.ds(c, 1), :][:, cols]
            e1c = e1_ref[hh, pl.ds(c, 1), :][:, cols]
            mask = (s2_ref[hh, :, cols] + s1c) >= tau_ref[hh, :, cols]
            term = jnp.where(mask, e2_ref[hh, :, cols] * e1c, 0.0)
            g = term if g is None else g + term
        a = a_rd[rows, cols]
        act = 0.5 * a * (1.0 + lax.erf(a * INV_SQRT2))
        wv = act * g
        w_wr[cols, rows] = wv.T.astype(BF16)
        folded = jnp.sum(wv.reshape(N_KEYS // 8, 8, LANES), axis=0)
        bits = pltpu.bitcast(folded, jnp.uint32)
        zero = lax.shift_right_logical(lax.shift_right_logical(bits, jnp.uint32(16)), jnp.uint32(16))
        return pltpu.bitcast(zero, F32)

    def apply_piece(w_rd, dc, zero):
        cols = slice(dc * dcol_w, (dc + 1) * dcol_w)
        w = w_rd[...]
        if zero is not None:
            zb = jnp.concatenate([zero, zero], axis=0).astype(BF16)
            w = w + jnp.tile(zb, (tm // 16, ec // LANES))
        o_ref[:, cols] += jnp.dot(w, v_ref[:, cols], preferred_element_type=F32)

    def step(a_wr, a_rd, w_wr, w_rd):
        proj = [functools.partial(project_piece, a_wr, tc) for tc in range(tm // tok_w)]
        gates = [functools.partial(gate_piece, a_rd, w_wr, cc, tc)
                 for cc in range(nc) for tc in range(tm // LANES)]
        applies = [functools.partial(apply_piece, w_rd, dc) for dc in range(d_model // dcol_w)]
        n_rounds = len(gates)
        zero = None
        for r in range(n_rounds):
            for q, piece in enumerate(proj):
                if q * n_rounds // len(proj) == r:
                    piece()
            prev_zero, zero = zero, gates[r]()
            for q, piece in enumerate(applies):
                if q * n_rounds // len(applies) == r:
                    piece(prev_zero)

    @pl.when(s % 2 == 0)
    def _():
        step(a0, a1, w1, w0)

    @pl.when(s % 2 == 1)
    def _():
        step(a1, a0, w0, w1)


def peer_experts(hf, u, v, s1, s2, e1, e2, tau, tm=512, ec=512):
    T, D = hf.shape
    NE = u.shape[0]
    tm = _tile(T, tm)
    ec = _tile(NE, ec)
    nc = ec // N_KEYS
    nk = NE // ec
    n_chunks = (T // tm) * nk
    last = n_chunks - 1
    once = pl.Buffered(1)

    def lag(s, d):
        return jnp.clip(s - d, 0, last)

    big_spec = pl.BlockSpec((PEER_HEADS, N_KEYS, tm), lambda s: (0, 0, lag(s, 1) // nk), pipeline_mode=once)
    return pl.pallas_call(
        functools.partial(_peer_expert_body, nc=nc, nk=nk, n_chunks=n_chunks),
        grid=(n_chunks + 2,),
        in_specs=[pl.BlockSpec((tm, D), lambda s: (lag(s, 0) // nk, 0), pipeline_mode=once),
                  pl.BlockSpec((ec, D), lambda s: (lag(s, 0) % nk, 0)),
                  pl.BlockSpec((ec, D), lambda s: (lag(s, 2) % nk, 0)),
                  big_spec, big_spec, big_spec, big_spec,
                  pl.BlockSpec((PEER_HEADS, 1, tm), lambda s: (0, 0, lag(s, 1) // nk), pipeline_mode=once)],
        out_specs=pl.BlockSpec((tm, D), lambda s: (lag(s, 2) // nk, 0)),
        out_shape=jax.ShapeDtypeStruct((T, D), F32),
        scratch_shapes=[pltpu.VMEM((ec, tm), F32), pltpu.VMEM((ec, tm), F32),
                        pltpu.VMEM((tm, ec), BF16), pltpu.VMEM((tm, ec), BF16)],
        compiler_params=_params("arbitrary"),
        name="peer_experts",
    )(hf, u, v, s1, s2, e1, e2, tau)


def _trunk(x, mem, w):
    B, S, D = x.shape
    M = mem.shape[1]
    T = B * S
    C = w["four_cols"]
    x2d = x.reshape(T, D)

    h = rmsnorm_rows(x2d, w["norm_mix"], BF16)
    proj = matmul(h, w["w_in"], BF16, name="in_proj")

    cos, sin = rope_tables(S)
    qk = qk_prep(proj, C, w["qk_gain"], cos, sin, S)
    o = gqa_attention(qk, proj, C + (N_HEADS + N_KV_HEADS) * HEAD_DIM, B, S)

    cs, ss = w["dft_tables"][S] if S in w["dft_tables"] else dft_tables(S)
    p = matmul(proj, w["dft_ch"], BF16, name="dft_channels")
    fm = dft_positions(cs, ss, p, B, S, C, 1.0 / math.sqrt(S * C))

    merged = branch_merge(o, fm, h, w["w_attn_br"], w["w_four_br"], w["w_gate"], w["b_gate"])
    x1 = matmul(merged, w["w_out"], F32, residual=x2d, name="out_proj")

    hc = rmsnorm_rows(x1, w["norm_ca"], BF16)
    qc = matmul(hc, w["w_cq"], BF16, scale=CA_HEAD_DIM ** -0.5, name="ca_q")
    mn = rmsnorm_rows(mem.reshape(B * M, D), w["mem_norm"], BF16)
    kv = matmul(mn, w["w_ckv"], BF16, name="ca_kv")
    oc = cross_attention(qc, kv, B, S, M)
    x2 = matmul(oc, w["w_co"], F32, residual=x1, name="ca_out")

    hf = rmsnorm_rows(x2, w["norm_ffn"], BF16)
    s1, s2, e1, e2, tau = peer_route(hf, w["w_pq_t"], w["sub_keys"])
    po = peer_experts(hf, w["expert_u"], w["expert_v"], s1, s2, e1, e2, tau)
    y = add_rmsnorm_rows(x2, po, w["final_norm"], F32)
    return y.reshape(B, S, D)


def kernel(x_prompt, x_sample, mem_prompt, mem_sample, norm_mix, w_in, q_norm, k_norm, w_attn_br, w_four_br, w_gate, b_gate, w_out, norm_ca, mem_norm, w_cq, w_ckv, w_co, norm_ffn, w_pq, sub_keys, expert_u, expert_v, final_norm):
    assert norm_mix.shape[0] == 1, "single-layer trunk"
    C = w_four_br.shape[1]
    attn_w = N_HEADS * HEAD_DIM
    kv_w = N_KV_HEADS * HEAD_DIM
    wi = w_in[0]
    w_in_r = jnp.concatenate([wi[:, attn_w + 2 * kv_w:], wi[:, :attn_w + 2 * kv_w]], axis=1).astype(BF16)
    cc, sc = dft_tables(C)
    dft_ch = jnp.concatenate([cc, sc], axis=1)
    scale = HEAD_DIM ** -0.5
    qk_gain = jnp.concatenate([jnp.tile(q_norm[0] * scale, N_HEADS),
                               jnp.tile(k_norm[0], N_KV_HEADS)]).reshape(1, -1).astype(F32)
    w = dict(
        four_cols=C,
        norm_mix=norm_mix[0], w_in=w_in_r, qk_gain=qk_gain, dft_ch=dft_ch, dft_tables={C: (cc, sc)},
        w_attn_br=w_attn_br[0].astype(BF16), w_four_br=w_four_br[0].astype(BF16),
        w_gate=w_gate[0].astype(BF16), b_gate=b_gate[0], w_out=w_out[0].astype(BF16),
        norm_ca=norm_ca[0], mem_norm=mem_norm[0], w_cq=w_cq[0].astype(BF16),
        w_ckv=w_ckv[0].astype(BF16), w_co=w_co[0].astype(BF16), norm_ffn=norm_ffn[0],
        w_pq_t=w_pq[0].T.astype(BF16), sub_keys=sub_keys[0].astype(F32),
        expert_u=expert_u[0].astype(BF16), expert_v=expert_v[0].astype(BF16),
        final_norm=final_norm,
    )
    y_prompt = _trunk(x_prompt, mem_prompt, w)
    y_sample = _trunk(x_sample, mem_sample, w)
    return (y_prompt, y_sample)
```

```python
import functools
import math

import jax
import jax.numpy as jnp
from jax import lax
from jax.experimental import pallas as pl
from jax.experimental.pallas import tpu as pltpu

F32 = jnp.float32
BF16 = jnp.bfloat16
F8 = jnp.float8_e4m3fn
F8_MAX = 448.0
F8_OPERAND_TARGET = 256.0
PEER_W_SCALE = 16.0

N_HEADS = 16
N_KV_HEADS = 4
HEAD_DIM = 128
GQA_GROUP = N_HEADS // N_KV_HEADS
ROPE_THETA = 10000.0
GRID_W = 64
CA_HEADS = 4
CA_HEAD_DIM = 256
N_KEYS = 128
PEER_HEADS = 8
PEER_TOPK = 16
PEER_HALF = 128
EPS = 1e-6
INV_SQRT2 = 0.7071067811865476

V7X_VMEM_BYTES = 64 * 1024 * 1024
VMEM_LIMIT = V7X_VMEM_BYTES - 8 * 1024 * 1024
LANES = 128


def _tile(dim, pref):
    t = min(pref, dim)
    while dim % t:
        t //= 2
    return t


def _params(*sem):
    return pltpu.CompilerParams(dimension_semantics=sem, vmem_limit_bytes=VMEM_LIMIT)


def _rmsnorm_body(x_ref, g_ref, o_ref):
    x = x_ref[...].astype(F32)
    ms = jnp.mean(x * x, axis=-1, keepdims=True)
    o_ref[...] = (x * lax.rsqrt(ms + EPS) * g_ref[...]).astype(o_ref.dtype)


def rmsnorm_rows(x2d, gain, out_dtype, tm=256):
    T, D = x2d.shape
    tm = _tile(T, tm)
    return pl.pallas_call(
        _rmsnorm_body,
        grid=(T // tm,),
        in_specs=[pl.BlockSpec((tm, D), lambda i: (i, 0)),
                  pl.BlockSpec((1, D), lambda i: (0, 0))],
        out_specs=pl.BlockSpec((tm, D), lambda i: (i, 0)),
        out_shape=jax.ShapeDtypeStruct((T, D), out_dtype),
        compiler_params=_params("parallel"),
        name="rmsnorm_rows",
    )(x2d, gain.reshape(1, D).astype(F32))


def _add_rmsnorm_body(x_ref, y_ref, g_ref, o_ref):
    x = x_ref[...].astype(F32) + y_ref[...].astype(F32)
    ms = jnp.mean(x * x, axis=-1, keepdims=True)
    o_ref[...] = (x * lax.rsqrt(ms + EPS) * g_ref[...]).astype(o_ref.dtype)


def add_rmsnorm_rows(x2d, y2d, gain, out_dtype, tm=256):
    T, D = x2d.shape
    tm = _tile(T, tm)
    return pl.pallas_call(
        _add_rmsnorm_body,
        grid=(T // tm,),
        in_specs=[pl.BlockSpec((tm, D), lambda i: (i, 0)),
                  pl.BlockSpec((tm, D), lambda i: (i, 0)),
                  pl.BlockSpec((1, D), lambda i: (0, 0))],
        out_specs=pl.BlockSpec((tm, D), lambda i: (i, 0)),
        out_shape=jax.ShapeDtypeStruct((T, D), out_dtype),
        compiler_params=_params("parallel"),
        name="add_rmsnorm_rows",
    )(x2d, y2d, gain.reshape(1, D).astype(F32))


def _mm_body(a_ref, b_ref, o_ref, *, scale):
    acc = jnp.dot(a_ref[...], b_ref[...], preferred_element_type=F32)
    if scale != 1.0:
        acc = acc * scale
    o_ref[...] = acc.astype(o_ref.dtype)


def _mm_res_body(a_ref, b_ref, r_ref, o_ref):
    acc = jnp.dot(a_ref[...], b_ref[...], preferred_element_type=F32)
    o_ref[...] = (r_ref[...].astype(F32) + acc).astype(o_ref.dtype)


def matmul(a, b, out_dtype, *, residual=None, scale=1.0, tm=1024, tn=512, name="matmul"):
    M = a.shape[0]
    K, N = b.shape
    tm = _tile(M, tm)
    tn = _tile(N, tn)
    in_specs = [pl.BlockSpec((tm, K), lambda i, j: (i, 0)),
                pl.BlockSpec((K, tn), lambda i, j: (0, j))]
    args = [a, b]
    if residual is None:
        body = functools.partial(_mm_body, scale=scale)
    else:
        body = _mm_res_body
        in_specs.append(pl.BlockSpec((tm, tn), lambda i, j: (i, j)))
        args.append(residual)
    return pl.pallas_call(
        body,
        grid=(M // tm, N // tn),
        in_specs=in_specs,
        out_specs=pl.BlockSpec((tm, tn), lambda i, j: (i, j)),
        out_shape=jax.ShapeDtypeStruct((M, N), out_dtype),
        compiler_params=_params("parallel", "arbitrary"),
        name=name,
    )(*args)


def _qk_prep_body(x_ref, g_ref, cos_ref, sin_ref, o_ref, *, heads):
    cos = cos_ref[...]
    sin = sin_ref[...]
    lane = lax.broadcasted_iota(jnp.int32, cos.shape, 1)
    first_half = (lane // (HEAD_DIM // 4)) % 2 == 0
    for hh in range(heads):
        sl = slice(hh * HEAD_DIM, (hh + 1) * HEAD_DIM)
        x = x_ref[:, sl].astype(F32)
        ms = jnp.mean(x * x, axis=-1, keepdims=True)
        y = x * lax.rsqrt(ms + EPS) * g_ref[:, sl]
        up = pltpu.roll(y, HEAD_DIM - HEAD_DIM // 4, 1)
        down = pltpu.roll(y, HEAD_DIM // 4, 1)
        partner = jnp.where(first_half, up, down)
        o_ref[:, sl] = (y * cos + partner * sin).astype(o_ref.dtype)


def qk_prep(proj, col0, gains, cos, sin, seq_len, tm=512):
    T = proj.shape[0]
    width = gains.shape[1]
    tm = _tile(seq_len, tm)
    heads = next(n for n in (4, 2, 1) if col0 % (n * HEAD_DIM) == 0 and width % (n * HEAD_DIM) == 0)
    bw = heads * HEAD_DIM
    cb0 = col0 // bw
    spb = seq_len // tm
    return pl.pallas_call(
        functools.partial(_qk_prep_body, heads=heads),
        grid=(T // tm, width // bw),
        in_specs=[pl.BlockSpec((tm, bw), lambda i, j: (i, cb0 + j)),
                  pl.BlockSpec((1, bw), lambda i, j: (0, j)),
                  pl.BlockSpec((tm, HEAD_DIM), lambda i, j: (i % spb, 0)),
                  pl.BlockSpec((tm, HEAD_DIM), lambda i, j: (i % spb, 0))],
        out_specs=pl.BlockSpec((tm, bw), lambda i, j: (i, j)),
        out_shape=jax.ShapeDtypeStruct((T, width), BF16),
        compiler_params=_params("parallel", "arbitrary"),
        name="qk_prep",
    )(proj, gains, cos, sin)


def rope_tables(seq_len):
    half = HEAD_DIM // 2
    t = jnp.arange(seq_len)
    pos = jnp.stack([t // GRID_W, t % GRID_W], axis=-1).astype(F32)
    inv_freq = ROPE_THETA ** (-jnp.arange(0, half, 2, dtype=F32) / half)
    ang = pos[:, :, None] * inv_freq
    cos = jnp.cos(ang)
    sin = jnp.sin(ang)
    cos_full = jnp.stack([cos, cos], axis=2).reshape(seq_len, HEAD_DIM)
    sin_full = jnp.stack([-sin, sin], axis=2).reshape(seq_len, HEAD_DIM)
    return cos_full, sin_full


def _attn_body(q_ref, k_ref, v_ref, o_ref):
    k = k_ref[...]
    v = v_ref[...]
    for g in range(GQA_GROUP):
        sl = slice(g * HEAD_DIM, (g + 1) * HEAD_DIM)
        s = lax.dot_general(q_ref[:, sl], k, (((1,), (1,)), ((), ())),
                            preferred_element_type=F32)
        m = jnp.max(s, axis=-1, keepdims=True)
        p = jnp.exp(s - m)
        l = jnp.sum(p, axis=-1, keepdims=True)
        o = jnp.dot(p.astype(BF16), v, preferred_element_type=F32)
        o_ref[:, sl] = (o / l).astype(o_ref.dtype)


def gqa_attention(qk, proj, v_col0, B, S, tq=256):
    tq = _tile(S, tq)
    qk3 = qk.reshape(B, S, qk.shape[1])
    proj3 = proj.reshape(B, S, proj.shape[1])
    gw = GQA_GROUP * HEAD_DIM
    kb0 = (N_HEADS * HEAD_DIM) // HEAD_DIM
    vb0 = v_col0 // HEAD_DIM
    out = pl.pallas_call(
        _attn_body,
        grid=(B, N_KV_HEADS, S // tq),
        in_specs=[pl.BlockSpec((None, tq, gw), lambda b, h, i: (b, i, h)),
                  pl.BlockSpec((None, S, HEAD_DIM), lambda b, h, i: (b, 0, kb0 + h)),
                  pl.BlockSpec((None, S, HEAD_DIM), lambda b, h, i: (b, 0, vb0 + h))],
        out_specs=pl.BlockSpec((None, tq, gw), lambda b, h, i: (b, i, h)),
        out_shape=jax.ShapeDtypeStruct((B, S, N_HEADS * HEAD_DIM), BF16),
        compiler_params=_params("parallel", "parallel", "arbitrary"),
        name="gqa_attention",
    )(qk3, qk3, proj3)
    return out.reshape(B * S, N_HEADS * HEAD_DIM)


def dft_tables(n, dtype=BF16):
    r = 64 if n % 64 == 0 else 1
    k = jnp.arange(n, dtype=jnp.int32)[None, :]

    def thin(rows):
        ang = ((rows[:, None] * k) % n).astype(F32) * (2.0 * math.pi / n)
        return jnp.cos(ang), jnp.sin(ang)

    c_hi, s_hi = thin(jnp.arange(n // r, dtype=jnp.int32) * r)
    c_lo, s_lo = thin(jnp.arange(r, dtype=jnp.int32))
    cos = c_hi[:, None, :] * c_lo[None, :, :] - s_hi[:, None, :] * s_lo[None, :, :]
    sin = s_hi[:, None, :] * c_lo[None, :, :] + c_hi[:, None, :] * s_lo[None, :, :]
    return cos.reshape(n, n).astype(dtype), sin.reshape(n, n).astype(dtype)


def _dft2_body(cs_ref, ss_ref, pc_ref, ps_ref, o_ref, *, scale):
    acc = jnp.dot(cs_ref[...], pc_ref[...], preferred_element_type=F32)
    acc = acc - jnp.dot(ss_ref[...], ps_ref[...], preferred_element_type=F32)
    o_ref[...] = (acc * scale).astype(o_ref.dtype)


def dft_positions(cs, ss, p, B, S, C, scale, tm=512, tn=512):
    tm = _tile(S, tm)
    tn = _tile(C, tn)
    p3 = p.reshape(B, S, 2 * C)
    nj = C // tn
    out = pl.pallas_call(
        functools.partial(_dft2_body, scale=scale),
        grid=(S // tm, B, nj),
        in_specs=[pl.BlockSpec((tm, S), lambda i, b, j: (i, 0)),
                  pl.BlockSpec((tm, S), lambda i, b, j: (i, 0)),
                  pl.BlockSpec((None, S, tn), lambda i, b, j: (b, 0, j)),
                  pl.BlockSpec((None, S, tn), lambda i, b, j: (b, 0, nj + j))],
        out_specs=pl.BlockSpec((None, tm, tn), lambda i, b, j: (b, i, j)),
        out_shape=jax.ShapeDtypeStruct((B, S, C), BF16),
        compiler_params=_params("parallel", "arbitrary", "arbitrary"),
        name="dft_positions",
    )(cs, ss, p3, p3)
    return out.reshape(B * S, C)


def _merge_body(o_ref, f_ref, h_ref, wa_ref, wf_ref, wg0_ref, wg1_ref, b0_ref, b1_ref, out_ref):
    h = h_ref[...]
    a_br = jnp.dot(o_ref[...], wa_ref[...], preferred_element_type=F32)
    f_br = jnp.dot(f_ref[...], wf_ref[...], preferred_element_type=F32)
    g0 = jax.nn.sigmoid(jnp.dot(h, wg0_ref[...], preferred_element_type=F32) + b0_ref[...])
    g1 = jax.nn.sigmoid(jnp.dot(h, wg1_ref[...], preferred_element_type=F32) + b1_ref[...])
    out_ref[...] = (g0 * a_br + g1 * f_br).astype(out_ref.dtype)


def branch_merge(o, fm, h, wa, wf, wg, bg, tm=512, tn=512):
    T, D = h.shape
    tm = _tile(T, tm)
    tn = _tile(D, tn)
    nj = D // tn
    ka = o.shape[1]
    kf = fm.shape[1]
    bg2 = bg.reshape(1, 2 * D).astype(F32)
    return pl.pallas_call(
        _merge_body,
        grid=(T // tm, nj),
        in_specs=[pl.BlockSpec((tm, ka), lambda i, j: (i, 0)),
                  pl.BlockSpec((tm, kf), lambda i, j: (i, 0)),
                  pl.BlockSpec((tm, D), lambda i, j: (i, 0)),
                  pl.BlockSpec((ka, tn), lambda i, j: (0, j)),
                  pl.BlockSpec((kf, tn), lambda i, j: (0, j)),
                  pl.BlockSpec((D, tn), lambda i, j: (0, j)),
                  pl.BlockSpec((D, tn), lambda i, j: (0, nj + j)),
                  pl.BlockSpec((1, tn), lambda i, j: (0, j)),
                  pl.BlockSpec((1, tn), lambda i, j: (0, nj + j))],
        out_specs=pl.BlockSpec((tm, tn), lambda i, j: (i, j)),
        out_shape=jax.ShapeDtypeStruct((T, D), BF16),
        compiler_params=_params("parallel", "arbitrary"),
        name="branch_merge",
    )(o, fm, h, wa, wf, wg, wg, bg2, bg2)


def _cross_attn_body(q_ref, kv_ref, o_ref):
    w = CA_HEADS * CA_HEAD_DIM
    for hh in range(CA_HEADS):
        sl = slice(hh * CA_HEAD_DIM, (hh + 1) * CA_HEAD_DIM)
        k = kv_ref[:, sl]
        v = kv_ref[:, w + hh * CA_HEAD_DIM: w + (hh + 1) * CA_HEAD_DIM]
        s = lax.dot_general(q_ref[:, sl], k, (((1,), (1,)), ((), ())),
                            preferred_element_type=F32)
        m = jnp.max(s, axis=-1, keepdims=True)
        p = jnp.exp(s - m)
        l = jnp.sum(p, axis=-1, keepdims=True)
        o = jnp.dot(p.astype(BF16), v, preferred_element_type=F32)
        o_ref[:, sl] = (o / l).astype(o_ref.dtype)


def cross_attention(qc, kv, B, S, M, tq=512):
    tq = _tile(S, tq)
    w = CA_HEADS * CA_HEAD_DIM
    out = pl.pallas_call(
        _cross_attn_body,
        grid=(B, S // tq),
        in_specs=[pl.BlockSpec((None, tq, w), lambda b, i: (b, i, 0)),
                  pl.BlockSpec((None, M, 2 * w), lambda b, i: (b, 0, 0))],
        out_specs=pl.BlockSpec((None, tq, w), lambda b, i: (b, i, 0)),
        out_shape=jax.ShapeDtypeStruct((B, S, w), BF16),
        compiler_params=_params("parallel", "arbitrary"),
        name="cross_attention",
    )(qc.reshape(B, S, w), kv.reshape(B, M, 2 * w))
    return out.reshape(B * S, w)


SUBLANES = 8


def _sort_network(n):
    pairs = []

    def merge(lo, length, r):
        step = r * 2
        if step < length:
            merge(lo, length, step)
            merge(lo + r, length, step)
            pairs.extend((i, i + r) for i in range(lo + r, lo + length - r, step))
        else:
            pairs.append((lo, lo + r))

    def sort(lo, length):
        if length > 1:
            half = length // 2
            sort(lo, half)
            sort(lo + half, half)
            merge(lo, length, 1)

    sort(0, 16)
    return [(i, j) for i, j in pairs if j < n]


def _compare_exchange(v, i, j):
    v[i], v[j] = jnp.maximum(v[i], v[j]), jnp.minimum(v[i], v[j])


def _bitonic_sort16(v):
    stride = 8
    while stride:
        for i in range(16):
            if not i & stride:
                _compare_exchange(v, i, i + stride)
        stride //= 2


def _merge_sublanes(v, shift, n_valid=16):
    def other(i):
        return pltpu.roll(v[i], shift, 0)

    out = []
    for i in range(16):
        j = 15 - i
        if i < n_valid and j < n_valid:
            out.append(jnp.maximum(v[i], other(j)))
        elif i < n_valid:
            out.append(v[i])
        else:
            out.append(other(j))
    return out


def _top16_sorted(s):
    v = [s[r * SUBLANES:(r + 1) * SUBLANES, :] for r in range(16)]
    for i, j in _sort_network(16):
        _compare_exchange(v, i, j)
    for shift in (4, 2, 1):
        v = _merge_sublanes(v, shift)
        _bitonic_sort16(v)
    return v


def _pack_sublanes(vals):
    row = lax.broadcasted_iota(jnp.int32, vals[0].shape, 0)
    out = vals[SUBLANES - 1]
    for r in range(SUBLANES - 2, -1, -1):
        out = jnp.where(row == r, vals[r], out)
    return out


def _pair_threshold(t1, t2):
    p_lo = _pack_sublanes(t2[:SUBLANES])
    p_hi = _pack_sublanes(t2[SUBLANES:])
    q_hi = _pack_sublanes(t1[SUBLANES:])
    cand = [t1[0] + p_lo, t1[0] + p_hi] + [t1[a] + p_lo for a in range(1, SUBLANES)] + [q_hi + t2[0]]
    v = list(cand)
    n = len(v)
    for i, j in _sort_network(n):
        _compare_exchange(v, i, j)
    v = _merge_sublanes(v, 4, n_valid=n)
    _bitonic_sort16(v)
    v = _merge_sublanes(v, 2)
    _bitonic_sort16(v)
    v = _merge_sublanes(v, 1)
    tau = functools.reduce(jnp.minimum, v)
    return tau, cand


def _peer_route_body(h_ref, wq_ref, keys_ref, s1_ref, s2_ref, e1_ref, e2_ref, tau_ref, q_scr, *, tl):
    q_scr[...] = lax.dot_general(wq_ref[...], h_ref[...], (((1,), (1,)), ((), ())),
                                 preferred_element_type=F32)
    tm = h_ref.shape[0]
    k1 = keys_ref[0]
    k2 = keys_ref[1]

    def per_head(hh, carry):
        r0 = pl.multiple_of(hh * (2 * PEER_HALF), 2 * PEER_HALF)
        for lc in range(tm // tl):
            ls = slice(lc * tl, (lc + 1) * tl)
            s1 = jnp.dot(k1, q_scr[pl.ds(r0, PEER_HALF), ls], preferred_element_type=F32,
                         precision=lax.Precision.HIGHEST)
            s2 = jnp.dot(k2, q_scr[pl.ds(r0 + PEER_HALF, PEER_HALF), ls],
                         preferred_element_type=F32, precision=lax.Precision.HIGHEST)
            t1 = _top16_sorted(s1)
            t2 = _top16_sorted(s2)
            tau, cand = _pair_threshold(t1, t2)
            m1 = t1[0][:1]
            m2 = t2[0][:1]
            top = t1[0] + t2[0]
            zs = [jnp.where(c >= tau, jnp.exp(c - top), 0.0) for c in cand]
            z = jnp.sum(functools.reduce(lambda x, y: x + y, zs), axis=0, keepdims=True)
            s1_ref[hh, :, ls] = s1
            s2_ref[hh, :, ls] = s2
            e1_ref[hh, :, ls] = jnp.exp(s1 - m1)
            e2_ref[hh, :, ls] = jnp.exp(s2 - m2) / z
            tau_ref[hh, :, ls] = tau[:1]
        return carry

    lax.fori_loop(0, PEER_HEADS, per_head, 0)


def peer_route(hf, wq_t, keys, tm=256, tl=128):
    T, D = hf.shape
    tm = _tile(T, tm)
    tl = _tile(tm, tl)
    qw = wq_t.shape[0]
    big = jax.ShapeDtypeStruct((PEER_HEADS, N_KEYS, T), F32)
    big_spec = pl.BlockSpec((PEER_HEADS, N_KEYS, tm), lambda i: (0, 0, i))
    return pl.pallas_call(
        functools.partial(_peer_route_body, tl=tl),
        grid=(T // tm,),
        in_specs=[pl.BlockSpec((tm, D), lambda i: (i, 0)),
                  pl.BlockSpec((qw, D), lambda i: (0, 0)),
                  pl.BlockSpec((2, N_KEYS, PEER_HALF), lambda i: (0, 0, 0))],
        out_specs=[big_spec, big_spec, big_spec, big_spec,
                   pl.BlockSpec((PEER_HEADS, 1, tm), lambda i: (0, 0, i))],
        out_shape=[big, big, big, big, jax.ShapeDtypeStruct((PEER_HEADS, 1, T), F32)],
        scratch_shapes=[pltpu.VMEM((qw, tm), F32)],
        compiler_params=_params("parallel"),
        name="peer_route",
    )(hf, wq_t, keys)


def _peer_expert_body(scale_ref, h_ref, u_ref, v_ref, s1_ref, s2_ref, e1_ref, e2_ref, tau_ref, o_ref,
                      a0, a1, w0, w1, *, nc, nk, n_chunks):
    s = pl.program_id(0)

    @pl.when(s == 0)
    def _():
        for ref in (a0, a1, w0, w1):
            ref[...] = jnp.zeros_like(ref)

    @pl.when((s == 0) | ((s >= 2) & ((s - 2) % nk == 0)))
    def _():
        o_ref[...] = jnp.zeros_like(o_ref)

    ec, tm = a0.shape
    d_model = o_ref.shape[1]
    chunk = jnp.clip(s - 1, 0, n_chunks - 1) % nk
    tok_w = min(tm, 2 * LANES)
    dcol_w = min(d_model, 2 * LANES)
    inv_u_scale = scale_ref[0]

    def project_piece(a_wr, tc):
        cols = slice(tc * tok_w, (tc + 1) * tok_w)
        a_wr[:, cols] = lax.dot_general(u_ref[...], h_ref[cols, :], (((1,), (1,)), ((), ())),
                                        preferred_element_type=F32)

    def gate_piece(a_rd, w_wr, cc, tl):
        c = chunk * nc + cc
        rows = slice(cc * N_KEYS, (cc + 1) * N_KEYS)
        cols = slice(tl * LANES, (tl + 1) * LANES)
        g = None
        for hh in range(PEER_HEADS):
            s1c = s1_ref[hh, pl.ds(c, 1), :][:, cols]
            e1c = e1_ref[hh, pl.ds(c, 1), :][:, cols]
            mask = (s2_ref[hh, :, cols] + s1c) >= tau_ref[hh, :, cols]
            term = jnp.where(mask, e2_ref[hh, :, cols] * e1c, 0.0)
            g = term if g is None else g + term
        a = a_rd[rows, cols] * inv_u_scale
        act = (0.5 * PEER_W_SCALE) * a * (1.0 + lax.erf(a * INV_SQRT2))
        wv = lax.clamp(-F8_MAX, act * g, F8_MAX)
        w_wr[cols, rows] = wv.T.astype(F8)
        folded = jnp.sum(wv.reshape(N_KEYS // 8, 8, LANES), axis=0)
        bits = pltpu.bitcast(folded, jnp.uint32)
        zero = lax.shift_right_logical(lax.shift_right_logical(bits, jnp.uint32(16)), jnp.uint32(16))
        return pltpu.bitcast(zero, F32)

    def apply_piece(w_rd, dc, zero):
        cols = slice(dc * dcol_w, (dc + 1) * dcol_w)
        acc = jnp.dot(w_rd[...], v_ref[:, cols], preferred_element_type=F32)
        if zero is not None:
            acc = acc + jnp.tile(zero, (tm // 8, dcol_w // LANES))
        o_ref[:, cols] += acc

    def step(a_wr, a_rd, w_wr, w_rd):
        proj = [functools.partial(project_piece, a_wr, tc) for tc in range(tm // tok_w)]
        gates = [functools.partial(gate_piece, a_rd, w_wr, cc, tc)
                 for cc in range(nc) for tc in range(tm // LANES)]
        applies = [functools.partial(apply_piece, w_rd, dc) for dc in range(d_model // dcol_w)]
        n_rounds = len(applies)
        prev_zero = None
        for r in range(n_rounds):
            for q, piece in enumerate(proj):
                if q * n_rounds // len(proj) == r:
                    piece()
            zeros = [piece() for q, piece in enumerate(gates) if q * n_rounds // len(gates) == r]
            applies[r](prev_zero)
            prev_zero = functools.reduce(lambda x, y: x + y, zeros)

    @pl.when(s % 2 == 0)
    def _():
        step(a0, a1, w1, w0)

    @pl.when(s % 2 == 1)
    def _():
        step(a1, a0, w0, w1)

    @pl.when((s >= 2) & ((s - 2) % nk == nk - 1))
    def _():
        o_ref[...] = o_ref[...] * scale_ref[1]


def _pow2_scale(x, target):
    m = jnp.max(jnp.abs(x)).astype(F32)
    p = jnp.exp2(jnp.floor(jnp.log2(target / jnp.maximum(m, jnp.finfo(F32).tiny))))
    return jnp.where(m > 0, jnp.clip(p, 2.0 ** -60, 2.0 ** 60), 1.0)


def peer_experts(hf8, u8, v8, scales, s1, s2, e1, e2, tau, tm=512, ec=512):
    T, D = hf8.shape
    NE = u8.shape[0]
    tm = _tile(T, tm)
    ec = _tile(NE, ec)
    nc = ec // N_KEYS
    nk = NE // ec
    n_chunks = (T // tm) * nk
    last = n_chunks - 1

    def lag(s, d):
        return jnp.clip(s - d, 0, last)

    big_spec = pl.BlockSpec((PEER_HEADS, N_KEYS, tm), lambda s: (0, 0, lag(s, 1) // nk))
    return pl.pallas_call(
        functools.partial(_peer_expert_body, nc=nc, nk=nk, n_chunks=n_chunks),
        grid=(n_chunks + 2,),
        in_specs=[pl.BlockSpec(memory_space=pltpu.SMEM),
                  pl.BlockSpec((tm, D), lambda s: (lag(s, 0) // nk, 0)),
                  pl.BlockSpec((ec, D), lambda s: (lag(s, 0) % nk, 0)),
                  pl.BlockSpec((ec, D), lambda s: (lag(s, 2) % nk, 0)),
                  big_spec, big_spec, big_spec, big_spec,
                  pl.BlockSpec((PEER_HEADS, 1, tm), lambda s: (0, 0, lag(s, 1) // nk))],
        out_specs=pl.BlockSpec((tm, D), lambda s: (lag(s, 2) // nk, 0)),
        out_shape=jax.ShapeDtypeStruct((T, D), F32),
        scratch_shapes=[pltpu.VMEM((ec, tm), F32), pltpu.VMEM((ec, tm), F32),
                        pltpu.VMEM((tm, ec), F8), pltpu.VMEM((tm, ec), F8)],
        compiler_params=_params("arbitrary"),
        name="peer_experts",
    )(scales, hf8, u8, v8, s1, s2, e1, e2, tau)


def _trunk(x, mem, w):
    B, S, D = x.shape
    M = mem.shape[1]
    T = B * S
    C = w["four_cols"]
    x2d = x.reshape(T, D)

    h = rmsnorm_rows(x2d, w["norm_mix"], BF16)
    proj = matmul(h, w["w_in"], BF16, name="in_proj")

    cos, sin = rope_tables(S)
    qk = qk_prep(proj, C, w["qk_gain"], cos, sin, S)
    o = gqa_attention(qk, proj, C + (N_HEADS + N_KV_HEADS) * HEAD_DIM, B, S)

    cs, ss = w["dft_tables"][S] if S in w["dft_tables"] else dft_tables(S)
    p = matmul(proj, w["dft_ch"], BF16, name="dft_channels")
    fm = dft_positions(cs, ss, p, B, S, C, 1.0 / math.sqrt(S * C))

    merged = branch_merge(o, fm, h, w["w_attn_br"], w["w_four_br"], w["w_gate"], w["b_gate"])
    x1 = matmul(merged, w["w_out"], F32, residual=x2d, name="out_proj")

    hc = rmsnorm_rows(x1, w["norm_ca"], BF16)
    qc = matmul(hc, w["w_cq"], BF16, scale=CA_HEAD_DIM ** -0.5, name="ca_q")
    mn = rmsnorm_rows(mem.reshape(B * M, D), w["mem_norm"], BF16)
    kv = matmul(mn, w["w_ckv"], BF16, name="ca_kv")
    oc = cross_attention(qc, kv, B, S, M)
    x2 = matmul(oc, w["w_co"], F32, residual=x1, name="ca_out")

    hf = rmsnorm_rows(x2, w["norm_ffn"], BF16)
    s1, s2, e1, e2, tau = peer_route(hf, w["w_pq_t"], w["sub_keys"])
    hf8 = (hf * w["peer_h_scale"].astype(BF16)).astype(F8)
    po = peer_experts(hf8, w["expert_u"], w["expert_v"], w["peer_scales"], s1, s2, e1, e2, tau)
    y = add_rmsnorm_rows(x2, po, w["final_norm"], F32)
    return y.reshape(B, S, D)


def kernel(x_prompt, x_sample, mem_prompt, mem_sample, norm_mix, w_in, q_norm, k_norm, w_attn_br, w_four_br, w_gate, b_gate, w_out, norm_ca, mem_norm, w_cq, w_ckv, w_co, norm_ffn, w_pq, sub_keys, expert_u, expert_v, final_norm):
    assert norm_mix.shape[0] == 1, "single-layer trunk"
    C = w_four_br.shape[1]
    attn_w = N_HEADS * HEAD_DIM
    kv_w = N_KV_HEADS * HEAD_DIM
    wi = w_in[0]
    w_in_r = jnp.concatenate([wi[:, attn_w + 2 * kv_w:], wi[:, :attn_w + 2 * kv_w]], axis=1).astype(BF16)
    cc, sc = dft_tables(C)
    dft_ch = jnp.concatenate([cc, sc], axis=1)
    scale = HEAD_DIM ** -0.5
    qk_gain = jnp.concatenate([jnp.tile(q_norm[0] * scale, N_HEADS),
                               jnp.tile(k_norm[0], N_KV_HEADS)]).reshape(1, -1).astype(F32)
    u_scale = _pow2_scale(expert_u[0], F8_OPERAND_TARGET)
    v_scale = _pow2_scale(expert_v[0], F8_OPERAND_TARGET)
    h_scale = _pow2_scale(norm_ffn[0] * math.sqrt(norm_ffn.shape[1]), F8_MAX)
    w = dict(
        four_cols=C,
        norm_mix=norm_mix[0], w_in=w_in_r, qk_gain=qk_gain, dft_ch=dft_ch, dft_tables={C: (cc, sc)},
        w_attn_br=w_attn_br[0].astype(BF16), w_four_br=w_four_br[0].astype(BF16),
        w_gate=w_gate[0].astype(BF16), b_gate=b_gate[0], w_out=w_out[0].astype(BF16),
        norm_ca=norm_ca[0], mem_norm=mem_norm[0], w_cq=w_cq[0].astype(BF16),
        w_ckv=w_ckv[0].astype(BF16), w_co=w_co[0].astype(BF16), norm_ffn=norm_ffn[0],
        w_pq_t=w_pq[0].T.astype(BF16), sub_keys=sub_keys[0].astype(F32),
        expert_u=(expert_u[0] * u_scale).astype(F8), expert_v=(expert_v[0] * v_scale).astype(F8),
        peer_h_scale=h_scale,
        peer_scales=jnp.stack([1.0 / (u_scale * h_scale), 1.0 / (PEER_W_SCALE * v_scale)]).astype(F32),
        final_norm=final_norm,
    )
    y_prompt = _trunk(x_prompt, mem_prompt, w)
    y_sample = _trunk(x_sample, mem_sample, w)
    return (y_prompt, y_sample)
```

```python
import functools
import math

import jax
import jax.numpy as jnp
from jax import lax
from jax.experimental import pallas as pl
from jax.experimental.pallas import tpu as pltpu

F32 = jnp.float32
BF16 = jnp.bfloat16
F8 = jnp.float8_e4m3fn
F8_MAX = 448.0
F8_OPERAND_TARGET = 256.0
PEER_W_SCALE = 16.0

N_HEADS = 16
N_KV_HEADS = 4
HEAD_DIM = 128
GQA_GROUP = N_HEADS // N_KV_HEADS
ROPE_THETA = 10000.0
GRID_W = 64
CA_HEADS = 4
CA_HEAD_DIM = 256
N_KEYS = 128
PEER_HEADS = 8
PEER_TOPK = 16
PEER_HALF = 128
EPS = 1e-6
INV_SQRT2 = 0.7071067811865476

V7X_VMEM_BYTES = 64 * 1024 * 1024
VMEM_LIMIT = V7X_VMEM_BYTES - 8 * 1024 * 1024
LANES = 128


def _tile(dim, pref):
    t = min(pref, dim)
    while dim % t:
        t //= 2
    return t


def _params(*sem):
    return pltpu.CompilerParams(dimension_semantics=sem, vmem_limit_bytes=VMEM_LIMIT)


def _rmsnorm_body(x_ref, g_ref, o_ref):
    x = x_ref[...].astype(F32)
    ms = jnp.mean(x * x, axis=-1, keepdims=True)
    o_ref[...] = (x * lax.rsqrt(ms + EPS) * g_ref[...]).astype(o_ref.dtype)


def rmsnorm_rows(x2d, gain, out_dtype, tm=256):
    T, D = x2d.shape
    tm = _tile(T, tm)
    return pl.pallas_call(
        _rmsnorm_body,
        grid=(T // tm,),
        in_specs=[pl.BlockSpec((tm, D), lambda i: (i, 0)),
                  pl.BlockSpec((1, D), lambda i: (0, 0))],
        out_specs=pl.BlockSpec((tm, D), lambda i: (i, 0)),
        out_shape=jax.ShapeDtypeStruct((T, D), out_dtype),
        compiler_params=_params("parallel"),
        name="rmsnorm_rows",
    )(x2d, gain.reshape(1, D).astype(F32))


def _add_rmsnorm_body(x_ref, y_ref, g_ref, o_ref):
    x = x_ref[...].astype(F32) + y_ref[...].astype(F32)
    ms = jnp.mean(x * x, axis=-1, keepdims=True)
    o_ref[...] = (x * lax.rsqrt(ms + EPS) * g_ref[...]).astype(o_ref.dtype)


def add_rmsnorm_rows(x2d, y2d, gain, out_dtype, tm=256):
    T, D = x2d.shape
    tm = _tile(T, tm)
    return pl.pallas_call(
        _add_rmsnorm_body,
        grid=(T // tm,),
        in_specs=[pl.BlockSpec((tm, D), lambda i: (i, 0)),
                  pl.BlockSpec((tm, D), lambda i: (i, 0)),
                  pl.BlockSpec((1, D), lambda i: (0, 0))],
        out_specs=pl.BlockSpec((tm, D), lambda i: (i, 0)),
        out_shape=jax.ShapeDtypeStruct((T, D), out_dtype),
        compiler_params=_params("parallel"),
        name="add_rmsnorm_rows",
    )(x2d, y2d, gain.reshape(1, D).astype(F32))


def _mm_body(a_ref, b_ref, o_ref, *, scale):
    acc = jnp.dot(a_ref[...], b_ref[...], preferred_element_type=F32)
    if scale != 1.0:
        acc = acc * scale
    o_ref[...] = acc.astype(o_ref.dtype)


def _mm_res_body(a_ref, b_ref, r_ref, o_ref):
    acc = jnp.dot(a_ref[...], b_ref[...], preferred_element_type=F32)
    o_ref[...] = (r_ref[...].astype(F32) + acc).astype(o_ref.dtype)


def matmul(a, b, out_dtype, *, residual=None, scale=1.0, tm=1024, tn=512, name="matmul"):
    M = a.shape[0]
    K, N = b.shape
    tm = _tile(M, tm)
    tn = _tile(N, tn)
    in_specs = [pl.BlockSpec((tm, K), lambda i, j: (i, 0)),
                pl.BlockSpec((K, tn), lambda i, j: (0, j))]
    args = [a, b]
    if residual is None:
        body = functools.partial(_mm_body, scale=scale)
    else:
        body = _mm_res_body
        in_specs.append(pl.BlockSpec((tm, tn), lambda i, j: (i, j)))
        args.append(residual)
    return pl.pallas_call(
        body,
        grid=(M // tm, N // tn),
        in_specs=in_specs,
        out_specs=pl.BlockSpec((tm, tn), lambda i, j: (i, j)),
        out_shape=jax.ShapeDtypeStruct((M, N), out_dtype),
        compiler_params=_params("parallel", "arbitrary"),
        name=name,
    )(*args)


def _qk_prep_body(x_ref, g_ref, cos_ref, sin_ref, o_ref, *, heads):
    cos = cos_ref[...]
    sin = sin_ref[...]
    lane = lax.broadcasted_iota(jnp.int32, cos.shape, 1)
    first_half = (lane // (HEAD_DIM // 4)) % 2 == 0
    for hh in range(heads):
        sl = slice(hh * HEAD_DIM, (hh + 1) * HEAD_DIM)
        x = x_ref[:, sl].astype(F32)
        ms = jnp.mean(x * x, axis=-1, keepdims=True)
        y = x * lax.rsqrt(ms + EPS) * g_ref[:, sl]
        up = pltpu.roll(y, HEAD_DIM - HEAD_DIM // 4, 1)
        down = pltpu.roll(y, HEAD_DIM // 4, 1)
        partner = jnp.where(first_half, up, down)
        o_ref[:, sl] = (y * cos + partner * sin).astype(o_ref.dtype)


def qk_prep(proj, col0, gains, cos, sin, seq_len, tm=512):
    T = proj.shape[0]
    width = gains.shape[1]
    tm = _tile(seq_len, tm)
    heads = next(n for n in (4, 2, 1) if col0 % (n * HEAD_DIM) == 0 and width % (n * HEAD_DIM) == 0)
    bw = heads * HEAD_DIM
    cb0 = col0 // bw
    spb = seq_len // tm
    return pl.pallas_call(
        functools.partial(_qk_prep_body, heads=heads),
        grid=(T // tm, width // bw),
        in_specs=[pl.BlockSpec((tm, bw), lambda i, j: (i, cb0 + j)),
                  pl.BlockSpec((1, bw), lambda i, j: (0, j)),
                  pl.BlockSpec((tm, HEAD_DIM), lambda i, j: (i % spb, 0)),
                  pl.BlockSpec((tm, HEAD_DIM), lambda i, j: (i % spb, 0))],
        out_specs=pl.BlockSpec((tm, bw), lambda i, j: (i, j)),
        out_shape=jax.ShapeDtypeStruct((T, width), BF16),
        compiler_params=_params("parallel", "arbitrary"),
        name="qk_prep",
    )(proj, gains, cos, sin)


def rope_tables(seq_len):
    half = HEAD_DIM // 2
    t = jnp.arange(seq_len)
    pos = jnp.stack([t // GRID_W, t % GRID_W], axis=-1).astype(F32)
    inv_freq = ROPE_THETA ** (-jnp.arange(0, half, 2, dtype=F32) / half)
    ang = pos[:, :, None] * inv_freq
    cos = jnp.cos(ang)
    sin = jnp.sin(ang)
    cos_full = jnp.stack([cos, cos], axis=2).reshape(seq_len, HEAD_DIM)
    sin_full = jnp.stack([-sin, sin], axis=2).reshape(seq_len, HEAD_DIM)
    return cos_full, sin_full


def _attn_body(q_ref, k_ref, v_ref, o_ref):
    k = k_ref[...]
    v = v_ref[...]
    for g in range(GQA_GROUP):
        sl = slice(g * HEAD_DIM, (g + 1) * HEAD_DIM)
        s = lax.dot_general(q_ref[:, sl], k, (((1,), (1,)), ((), ())),
                            preferred_element_type=F32)
        m = jnp.max(s, axis=-1, keepdims=True)
        p = jnp.exp(s - m)
        l = jnp.sum(p, axis=-1, keepdims=True)
        o = jnp.dot(p.astype(BF16), v, preferred_element_type=F32)
        o_ref[:, sl] = (o / l).astype(o_ref.dtype)


def gqa_attention(qk, proj, v_col0, B, S, tq=256):
    tq = _tile(S, tq)
    qk3 = qk.reshape(B, S, qk.shape[1])
    proj3 = proj.reshape(B, S, proj.shape[1])
    gw = GQA_GROUP * HEAD_DIM
    kb0 = (N_HEADS * HEAD_DIM) // HEAD_DIM
    vb0 = v_col0 // HEAD_DIM
    out = pl.pallas_call(
        _attn_body,
        grid=(B, N_KV_HEADS, S // tq),
        in_specs=[pl.BlockSpec((None, tq, gw), lambda b, h, i: (b, i, h)),
                  pl.BlockSpec((None, S, HEAD_DIM), lambda b, h, i: (b, 0, kb0 + h)),
                  pl.BlockSpec((None, S, HEAD_DIM), lambda b, h, i: (b, 0, vb0 + h))],
        out_specs=pl.BlockSpec((None, tq, gw), lambda b, h, i: (b, i, h)),
        out_shape=jax.ShapeDtypeStruct((B, S, N_HEADS * HEAD_DIM), BF16),
        compiler_params=_params("parallel", "parallel", "arbitrary"),
        name="gqa_attention",
    )(qk3, qk3, proj3)
    return out.reshape(B * S, N_HEADS * HEAD_DIM)


def dft_tables(n, dtype=BF16):
    r = 64 if n % 64 == 0 else 1
    k = jnp.arange(n, dtype=jnp.int32)[None, :]

    def thin(rows):
        ang = ((rows[:, None] * k) % n).astype(F32) * (2.0 * math.pi / n)
        return jnp.cos(ang), jnp.sin(ang)

    c_hi, s_hi = thin(jnp.arange(n // r, dtype=jnp.int32) * r)
    c_lo, s_lo = thin(jnp.arange(r, dtype=jnp.int32))
    cos = c_hi[:, None, :] * c_lo[None, :, :] - s_hi[:, None, :] * s_lo[None, :, :]
    sin = s_hi[:, None, :] * c_lo[None, :, :] + c_hi[:, None, :] * s_lo[None, :, :]
    return cos.reshape(n, n).astype(dtype), sin.reshape(n, n).astype(dtype)


def _dft2_body(cs_ref, ss_ref, pc_ref, ps_ref, o_ref, *, scale):
    acc = jnp.dot(cs_ref[...], pc_ref[...], preferred_element_type=F32)
    acc = acc - jnp.dot(ss_ref[...], ps_ref[...], preferred_element_type=F32)
    o_ref[...] = (acc * scale).astype(o_ref.dtype)


def dft_positions(cs, ss, p, B, S, C, scale, tm=512, tn=512):
    tm = _tile(S, tm)
    tn = _tile(C, tn)
    p3 = p.reshape(B, S, 2 * C)
    nj = C // tn
    out = pl.pallas_call(
        functools.partial(_dft2_body, scale=scale),
        grid=(S // tm, B, nj),
        in_specs=[pl.BlockSpec((tm, S), lambda i, b, j: (i, 0)),
                  pl.BlockSpec((tm, S), lambda i, b, j: (i, 0)),
                  pl.BlockSpec((None, S, tn), lambda i, b, j: (b, 0, j)),
                  pl.BlockSpec((None, S, tn), lambda i, b, j: (b, 0, nj + j))],
        out_specs=pl.BlockSpec((None, tm, tn), lambda i, b, j: (b, i, j)),
        out_shape=jax.ShapeDtypeStruct((B, S, C), BF16),
        compiler_params=_params("parallel", "arbitrary", "arbitrary"),
        name="dft_positions",
    )(cs, ss, p3, p3)
    return out.reshape(B * S, C)


def _merge_body(scale_ref, o_ref, f_ref, h_ref, wa_ref, wf_ref, wg0_ref, wg1_ref, b0_ref, b1_ref, out_ref,
                h8_scr):
    @pl.when(pl.program_id(1) == 0)
    def _():
        h8_scr[...] = (h_ref[...].astype(F32) * scale_ref[0]).astype(F8)

    h8 = h8_scr[...]
    a_br = jnp.dot(o_ref[...], wa_ref[...], preferred_element_type=F32)
    f_br = jnp.dot(f_ref[...], wf_ref[...], preferred_element_type=F32)
    z0 = jnp.dot(h8, wg0_ref[...], preferred_element_type=F32) * scale_ref[1] + b0_ref[...]
    z1 = jnp.dot(h8, wg1_ref[...], preferred_element_type=F32) * scale_ref[1] + b1_ref[...]
    out_ref[...] = (jax.nn.sigmoid(z0) * a_br + jax.nn.sigmoid(z1) * f_br).astype(out_ref.dtype)


def branch_merge(o, fm, h, wa, wf, wg, bg, scales, tm=512, tn=512):
    T, D = h.shape
    tm = _tile(T, tm)
    tn = _tile(D, tn)
    nj = D // tn
    ka = o.shape[1]
    kf = fm.shape[1]
    bg2 = bg.reshape(1, 2 * D).astype(F32)
    return pl.pallas_call(
        _merge_body,
        grid=(T // tm, nj),
        in_specs=[pl.BlockSpec(memory_space=pltpu.SMEM),
                  pl.BlockSpec((tm, ka), lambda i, j: (i, 0)),
                  pl.BlockSpec((tm, kf), lambda i, j: (i, 0)),
                  pl.BlockSpec((tm, D), lambda i, j: (i, 0)),
                  pl.BlockSpec((ka, tn), lambda i, j: (0, j)),
                  pl.BlockSpec((kf, tn), lambda i, j: (0, j)),
                  pl.BlockSpec((D, tn), lambda i, j: (0, j)),
                  pl.BlockSpec((D, tn), lambda i, j: (0, nj + j)),
                  pl.BlockSpec((1, tn), lambda i, j: (0, j)),
                  pl.BlockSpec((1, tn), lambda i, j: (0, nj + j))],
        out_specs=pl.BlockSpec((tm, tn), lambda i, j: (i, j)),
        out_shape=jax.ShapeDtypeStruct((T, D), BF16),
        scratch_shapes=[pltpu.VMEM((tm, D), F8)],
        compiler_params=_params("parallel", "arbitrary"),
        name="branch_merge",
    )(scales, o, fm, h, wa, wf, wg, wg, bg2, bg2)


def _cross_attn_body(q_ref, kv_ref, o_ref):
    w = CA_HEADS * CA_HEAD_DIM
    for hh in range(CA_HEADS):
        sl = slice(hh * CA_HEAD_DIM, (hh + 1) * CA_HEAD_DIM)
        k = kv_ref[:, sl]
        v = kv_ref[:, w + hh * CA_HEAD_DIM: w + (hh + 1) * CA_HEAD_DIM]
        s = lax.dot_general(q_ref[:, sl], k, (((1,), (1,)), ((), ())),
                            preferred_element_type=F32)
        m = jnp.max(s, axis=-1, keepdims=True)
        p = jnp.exp(s - m)
        l = jnp.sum(p, axis=-1, keepdims=True)
        o = jnp.dot(p.astype(BF16), v, preferred_element_type=F32)
        o_ref[:, sl] = (o / l).astype(o_ref.dtype)


def cross_attention(qc, kv, B, S, M, tq=512):
    tq = _tile(S, tq)
    w = CA_HEADS * CA_HEAD_DIM
    out = pl.pallas_call(
        _cross_attn_body,
        grid=(B, S // tq),
        in_specs=[pl.BlockSpec((None, tq, w), lambda b, i: (b, i, 0)),
                  pl.BlockSpec((None, M, 2 * w), lambda b, i: (b, 0, 0))],
        out_specs=pl.BlockSpec((None, tq, w), lambda b, i: (b, i, 0)),
        out_shape=jax.ShapeDtypeStruct((B, S, w), BF16),
        compiler_params=_params("parallel", "arbitrary"),
        name="cross_attention",
    )(qc.reshape(B, S, w), kv.reshape(B, M, 2 * w))
    return out.reshape(B * S, w)


SUBLANES = 8


def _sort_network(n):
    pairs = []

    def merge(lo, length, r):
        step = r * 2
        if step < length:
            merge(lo, length, step)
            merge(lo + r, length, step)
            pairs.extend((i, i + r) for i in range(lo + r, lo + length - r, step))
        else:
            pairs.append((lo, lo + r))

    def sort(lo, length):
        if length > 1:
            half = length // 2
            sort(lo, half)
            sort(lo + half, half)
            merge(lo, length, 1)

    sort(0, 16)
    return [(i, j) for i, j in pairs if j < n]


def _compare_exchange(v, i, j):
    v[i], v[j] = jnp.maximum(v[i], v[j]), jnp.minimum(v[i], v[j])


def _bitonic_sort16(v):
    stride = 8
    while stride:
        for i in range(16):
            if not i & stride:
                _compare_exchange(v, i, i + stride)
        stride //= 2


def _merge_sublanes(v, shift, n_valid=16):
    def other(i):
        return pltpu.roll(v[i], shift, 0)

    out = []
    for i in range(16):
        j = 15 - i
        if i < n_valid and j < n_valid:
            out.append(jnp.maximum(v[i], other(j)))
        elif i < n_valid:
            out.append(v[i])
        else:
            out.append(other(j))
    return out


def _top16_sorted(s):
    v = [s[r * SUBLANES:(r + 1) * SUBLANES, :] for r in range(16)]
    for i, j in _sort_network(16):
        _compare_exchange(v, i, j)
    for shift in (4, 2, 1):
        v = _merge_sublanes(v, shift)
        _bitonic_sort16(v)
    return v


def _pack_sublanes(vals):
    row = lax.broadcasted_iota(jnp.int32, vals[0].shape, 0)
    out = vals[SUBLANES - 1]
    for r in range(SUBLANES - 2, -1, -1):
        out = jnp.where(row == r, vals[r], out)
    return out


def _pair_threshold(t1, t2):
    p_lo = _pack_sublanes(t2[:SUBLANES])
    p_hi = _pack_sublanes(t2[SUBLANES:])
    q_hi = _pack_sublanes(t1[SUBLANES:])
    cand = [t1[0] + p_lo, t1[0] + p_hi] + [t1[a] + p_lo for a in range(1, SUBLANES)] + [q_hi + t2[0]]
    v = list(cand)
    n = len(v)
    for i, j in _sort_network(n):
        _compare_exchange(v, i, j)
    v = _merge_sublanes(v, 4, n_valid=n)
    _bitonic_sort16(v)
    v = _merge_sublanes(v, 2)
    _bitonic_sort16(v)
    v = _merge_sublanes(v, 1)
    tau = functools.reduce(jnp.minimum, v)
    return tau, cand


def _peer_route_body(h_ref, wq_ref, keys_ref, s1_ref, s2_ref, e1_ref, e2_ref, tau_ref, q_scr, *, tl):
    q_scr[...] = lax.dot_general(wq_ref[...], h_ref[...], (((1,), (1,)), ((), ())),
                                 preferred_element_type=F32)
    tm = h_ref.shape[0]
    k1 = keys_ref[0]
    k2 = keys_ref[1]

    def per_head(hh, carry):
        r0 = pl.multiple_of(hh * (2 * PEER_HALF), 2 * PEER_HALF)
        for lc in range(tm // tl):
            ls = slice(lc * tl, (lc + 1) * tl)
            s1 = jnp.dot(k1, q_scr[pl.ds(r0, PEER_HALF), ls], preferred_element_type=F32,
                         precision=lax.Precision.HIGHEST)
            s2 = jnp.dot(k2, q_scr[pl.ds(r0 + PEER_HALF, PEER_HALF), ls],
                         preferred_element_type=F32, precision=lax.Precision.HIGHEST)
            t1 = _top16_sorted(s1)
            t2 = _top16_sorted(s2)
            tau, cand = _pair_threshold(t1, t2)
            m1 = t1[0][:1]
            m2 = t2[0][:1]
            top = t1[0] + t2[0]
            zs = [jnp.where(c >= tau, jnp.exp(c - top), 0.0) for c in cand]
            z = jnp.sum(functools.reduce(lambda x, y: x + y, zs), axis=0, keepdims=True)
            s1_ref[hh, :, ls] = s1
            s2_ref[hh, :, ls] = s2
            e1_ref[hh, :, ls] = jnp.exp(s1 - m1)
            e2_ref[hh, :, ls] = jnp.exp(s2 - m2) / z
            tau_ref[hh, :, ls] = tau[:1]
        return carry

    lax.fori_loop(0, PEER_HEADS, per_head, 0)


def peer_route(hf, wq_t, keys, tm=256, tl=128):
    T, D = hf.shape
    tm = _tile(T, tm)
    tl = _tile(tm, tl)
    qw = wq_t.shape[0]
    big = jax.ShapeDtypeStruct((PEER_HEADS, N_KEYS, T), F32)
    big_spec = pl.BlockSpec((PEER_HEADS, N_KEYS, tm), lambda i: (0, 0, i))
    return pl.pallas_call(
        functools.partial(_peer_route_body, tl=tl),
        grid=(T // tm,),
        in_specs=[pl.BlockSpec((tm, D), lambda i: (i, 0)),
                  pl.BlockSpec((qw, D), lambda i: (0, 0)),
                  pl.BlockSpec((2, N_KEYS, PEER_HALF), lambda i: (0, 0, 0))],
        out_specs=[big_spec, big_spec, big_spec, big_spec,
                   pl.BlockSpec((PEER_HEADS, 1, tm), lambda i: (0, 0, i))],
        out_shape=[big, big, big, big, jax.ShapeDtypeStruct((PEER_HEADS, 1, T), F32)],
        scratch_shapes=[pltpu.VMEM((qw, tm), F32)],
        compiler_params=_params("parallel"),
        name="peer_route",
    )(hf, wq_t, keys)


def _peer_expert_body(scale_ref, h_ref, u_ref, v_ref, s1_ref, s2_ref, e1_ref, e2_ref, tau_ref, o_ref,
                      a0, a1, *, nc, nk, n_chunks):
    s = pl.program_id(0)

    @pl.when(s == 0)
    def _():
        for ref in (a0, a1):
            ref[...] = jnp.zeros_like(ref)

    @pl.when((s == 0) | ((s >= 1) & ((s - 1) % nk == 0)))
    def _():
        o_ref[...] = jnp.zeros_like(o_ref)

    ec, tm = a0.shape
    d_model = o_ref.shape[1]
    chunk = jnp.clip(s - 1, 0, n_chunks - 1) % nk
    tok_w = min(tm, 2 * LANES)
    dcol_w = min(d_model, 4 * LANES)
    inv_u_scale = scale_ref[0]

    def project_piece(a_wr, tc):
        cols = slice(tc * tok_w, (tc + 1) * tok_w)
        a_wr[:, cols] = lax.dot_general(u_ref[...], h_ref[cols, :], (((1,), (1,)), ((), ())),
                                        preferred_element_type=F32)

    def gate_piece(a_rd, cc, tl):
        c = chunk * nc + cc
        rows = slice(cc * N_KEYS, (cc + 1) * N_KEYS)
        cols = slice(tl * LANES, (tl + 1) * LANES)
        g = None
        for hh in range(PEER_HEADS):
            s1c = s1_ref[hh, pl.ds(c, 1), :][:, cols]
            e1c = e1_ref[hh, pl.ds(c, 1), :][:, cols]
            mask = (s2_ref[hh, :, cols] + s1c) >= tau_ref[hh, :, cols]
            term = jnp.where(mask, e2_ref[hh, :, cols] * e1c, 0.0)
            g = term if g is None else g + term
        a = a_rd[rows, cols] * inv_u_scale
        act = (0.5 * PEER_W_SCALE) * a * (1.0 + lax.erf(a * INV_SQRT2))
        wv = lax.clamp(-F8_MAX, act * g, F8_MAX)
        return wv.T.astype(F8)

    def apply_piece(half, w_half, dc):
        tok = slice(half * tok_w, (half + 1) * tok_w)
        cols = slice(dc * dcol_w, (dc + 1) * dcol_w)
        o_ref[tok, cols] += jnp.dot(w_half, v_ref[:, cols], preferred_element_type=F32)

    def step(a_wr, a_rd):
        lanes_per_half = tok_w // LANES
        for tc in range(tm // tok_w):
            project_piece(a_wr, tc)
        for half in range(tm // tok_w):
            tiles = [[gate_piece(a_rd, cc, half * lanes_per_half + tl) for cc in range(nc)]
                     for tl in range(lanes_per_half)]
            w_half = jnp.concatenate([jnp.concatenate(row, axis=1) for row in tiles], axis=0)
            for dc in range(d_model // dcol_w):
                apply_piece(half, w_half, dc)

    @pl.when(s % 2 == 0)
    def _():
        step(a0, a1)

    @pl.when(s % 2 == 1)
    def _():
        step(a1, a0)

    @pl.when((s >= 1) & ((s - 1) % nk == nk - 1))
    def _():
        o_ref[...] = o_ref[...] * scale_ref[1]


def _pow2_scale(x, target):
    m = jnp.max(jnp.abs(x)).astype(F32)
    p = jnp.exp2(jnp.floor(jnp.log2(target / jnp.maximum(m, jnp.finfo(F32).tiny))))
    return jnp.where(m > 0, jnp.clip(p, 2.0 ** -60, 2.0 ** 60), 1.0)


def peer_experts(hf8, u8, v8, scales, s1, s2, e1, e2, tau, tm=512, ec=512):
    T, D = hf8.shape
    NE = u8.shape[0]
    tm = _tile(T, tm)
    ec = _tile(NE, ec)
    nc = ec // N_KEYS
    nk = NE // ec
    n_chunks = (T // tm) * nk
    last = n_chunks - 1

    def lag(s, d):
        return jnp.clip(s - d, 0, last)

    big_spec = pl.BlockSpec((PEER_HEADS, N_KEYS, tm), lambda s: (0, 0, lag(s, 1) // nk))
    return pl.pallas_call(
        functools.partial(_peer_expert_body, nc=nc, nk=nk, n_chunks=n_chunks),
        grid=(n_chunks + 1,),
        in_specs=[pl.BlockSpec(memory_space=pltpu.SMEM),
                  pl.BlockSpec((tm, D), lambda s: (lag(s, 0) // nk, 0)),
                  pl.BlockSpec((ec, D), lambda s: (lag(s, 0) % nk, 0)),
                  pl.BlockSpec((ec, D), lambda s: (lag(s, 1) % nk, 0)),
                  big_spec, big_spec, big_spec, big_spec,
                  pl.BlockSpec((PEER_HEADS, 1, tm), lambda s: (0, 0, lag(s, 1) // nk))],
        out_specs=pl.BlockSpec((tm, D), lambda s: (lag(s, 1) // nk, 0)),
        out_shape=jax.ShapeDtypeStruct((T, D), F32),
        scratch_shapes=[pltpu.VMEM((ec, tm), F32), pltpu.VMEM((ec, tm), F32)],
        compiler_params=_params("arbitrary"),
        name="peer_experts",
    )(scales, hf8, u8, v8, s1, s2, e1, e2, tau)


def _trunk(x, mem, w):
    B, S, D = x.shape
    M = mem.shape[1]
    T = B * S
    C = w["four_cols"]
    x2d = x.reshape(T, D)

    h = rmsnorm_rows(x2d, w["norm_mix"], BF16)
    proj = matmul(h, w["w_in"], BF16, name="in_proj")

    cos, sin = rope_tables(S)
    qk = qk_prep(proj, C, w["qk_gain"], cos, sin, S)
    o = gqa_attention(qk, proj, C + (N_HEADS + N_KV_HEADS) * HEAD_DIM, B, S)

    cs, ss = w["dft_tables"][S] if S in w["dft_tables"] else dft_tables(S)
    p = matmul(proj, w["dft_ch"], BF16, name="dft_channels")
    fm = dft_positions(cs, ss, p, B, S, C, 1.0 / math.sqrt(S * C))

    merged = branch_merge(o, fm, h, w["w_attn_br"], w["w_four_br"], w["w_gate"], w["b_gate"],
                          w["gate_scales"])
    x1 = matmul(merged, w["w_out"], F32, residual=x2d, name="out_proj")

    hc = rmsnorm_rows(x1, w["norm_ca"], BF16)
    qc = matmul(hc, w["w_cq"], BF16, scale=CA_HEAD_DIM ** -0.5, name="ca_q")
    mn = rmsnorm_rows(mem.reshape(B * M, D), w["mem_norm"], BF16)
    kv = matmul(mn, w["w_ckv"], BF16, name="ca_kv")
    oc = cross_attention(qc, kv, B, S, M)
    x2 = matmul(oc, w["w_co"], F32, residual=x1, name="ca_out")

    hf = rmsnorm_rows(x2, w["norm_ffn"], BF16)
    s1, s2, e1, e2, tau = peer_route(hf, w["w_pq_t"], w["sub_keys"])
    hf8 = (hf * w["peer_h_scale"].astype(BF16)).astype(F8)
    po = peer_experts(hf8, w["expert_u"], w["expert_v"], w["peer_scales"], s1, s2, e1, e2, tau)
    y = add_rmsnorm_rows(x2, po, w["final_norm"], F32)
    return y.reshape(B, S, D)


def kernel(x_prompt, x_sample, mem_prompt, mem_sample, norm_mix, w_in, q_norm, k_norm, w_attn_br, w_four_br, w_gate, b_gate, w_out, norm_ca, mem_norm, w_cq, w_ckv, w_co, norm_ffn, w_pq, sub_keys, expert_u, expert_v, final_norm):
    assert norm_mix.shape[0] == 1, "single-layer trunk"
    C = w_four_br.shape[1]
    attn_w = N_HEADS * HEAD_DIM
    kv_w = N_KV_HEADS * HEAD_DIM
    wi = w_in[0]
    w_in_r = jnp.concatenate([wi[:, attn_w + 2 * kv_w:], wi[:, :attn_w + 2 * kv_w]], axis=1).astype(BF16)
    cc, sc = dft_tables(C)
    dft_ch = jnp.concatenate([cc, sc], axis=1)
    scale = HEAD_DIM ** -0.5
    qk_gain = jnp.concatenate([jnp.tile(q_norm[0] * scale, N_HEADS),
                               jnp.tile(k_norm[0], N_KV_HEADS)]).reshape(1, -1).astype(F32)
    g_scale = _pow2_scale(w_gate[0], F8_OPERAND_TARGET)
    hm_scale = _pow2_scale(norm_mix[0] * math.sqrt(norm_mix.shape[1]), F8_OPERAND_TARGET)
    u_scale = _pow2_scale(expert_u[0], F8_OPERAND_TARGET)
    v_scale = _pow2_scale(expert_v[0], F8_OPERAND_TARGET)
    h_scale = _pow2_scale(norm_ffn[0] * math.sqrt(norm_ffn.shape[1]), F8_OPERAND_TARGET)
    w = dict(
        four_cols=C,
        norm_mix=norm_mix[0], w_in=w_in_r, qk_gain=qk_gain, dft_ch=dft_ch, dft_tables={C: (cc, sc)},
        w_attn_br=w_attn_br[0].astype(BF16), w_four_br=w_four_br[0].astype(BF16),
        w_gate=(w_gate[0] * g_scale).astype(F8), b_gate=b_gate[0], w_out=w_out[0].astype(BF16),
        gate_scales=jnp.stack([hm_scale, 1.0 / (hm_scale * g_scale)]).astype(F32),
        norm_ca=norm_ca[0], mem_norm=mem_norm[0], w_cq=w_cq[0].astype(BF16),
        w_ckv=w_ckv[0].astype(BF16), w_co=w_co[0].astype(BF16), norm_ffn=norm_ffn[0],
        w_pq_t=w_pq[0].T.astype(BF16), sub_keys=sub_keys[0].astype(F32),
        expert_u=(expert_u[0] * u_scale).astype(F8), expert_v=(expert_v[0] * v_scale).astype(F8),
        peer_h_scale=h_scale,
        peer_scales=jnp.stack([1.0 / (u_scale * h_scale), 1.0 / (PEER_W_SCALE * v_scale)]).astype(F32),
        final_norm=final_norm,
    )
    y_prompt = _trunk(x_prompt, mem_prompt, w)
    y_sample = _trunk(x_sample, mem_sample, w)
    return (y_prompt, y_sample)
```

```python
import functools
import math

import jax
import jax.numpy as jnp
from jax import lax
from jax.experimental import pallas as pl
from jax.experimental.pallas import tpu as pltpu

F32 = jnp.float32
BF16 = jnp.bfloat16
F8 = jnp.float8_e4m3fn
F8_MAX = 448.0
F8_OPERAND_TARGET = 256.0
PEER_W_SCALE = 16.0

N_HEADS = 16
N_KV_HEADS = 4
HEAD_DIM = 128
GQA_GROUP = N_HEADS // N_KV_HEADS
ROPE_THETA = 10000.0
GRID_W = 64
CA_HEADS = 4
CA_HEAD_DIM = 256
N_KEYS = 128
PEER_HEADS = 8
PEER_TOPK = 16
PEER_HALF = 128
EPS = 1e-6
INV_SQRT2 = 0.7071067811865476

V7X_VMEM_BYTES = 64 * 1024 * 1024
VMEM_LIMIT = V7X_VMEM_BYTES - 8 * 1024 * 1024
LANES = 128


def _tile(dim, pref):
    t = min(pref, dim)
    while dim % t:
        t //= 2
    return t


def _params(*sem):
    return pltpu.CompilerParams(dimension_semantics=sem, vmem_limit_bytes=VMEM_LIMIT)


def _rmsnorm_body(x_ref, g_ref, o_ref):
    x = x_ref[...].astype(F32)
    ms = jnp.mean(x * x, axis=-1, keepdims=True)
    o_ref[...] = (x * lax.rsqrt(ms + EPS) * g_ref[...]).astype(o_ref.dtype)


def rmsnorm_rows(x2d, gain, out_dtype, tm=256):
    T, D = x2d.shape
    tm = _tile(T, tm)
    return pl.pallas_call(
        _rmsnorm_body,
        grid=(T // tm,),
        in_specs=[pl.BlockSpec((tm, D), lambda i: (i, 0)),
                  pl.BlockSpec((1, D), lambda i: (0, 0))],
        out_specs=pl.BlockSpec((tm, D), lambda i: (i, 0)),
        out_shape=jax.ShapeDtypeStruct((T, D), out_dtype),
        compiler_params=_params("parallel"),
        name="rmsnorm_rows",
    )(x2d, gain.reshape(1, D).astype(F32))


def _add_rmsnorm_body(x_ref, y_ref, g_ref, o_ref):
    x = x_ref[...].astype(F32) + y_ref[...].astype(F32)
    ms = jnp.mean(x * x, axis=-1, keepdims=True)
    o_ref[...] = (x * lax.rsqrt(ms + EPS) * g_ref[...]).astype(o_ref.dtype)


def add_rmsnorm_rows(x2d, y2d, gain, out_dtype, tm=256):
    T, D = x2d.shape
    tm = _tile(T, tm)
    return pl.pallas_call(
        _add_rmsnorm_body,
        grid=(T // tm,),
        in_specs=[pl.BlockSpec((tm, D), lambda i: (i, 0)),
                  pl.BlockSpec((tm, D), lambda i: (i, 0)),
                  pl.BlockSpec((1, D), lambda i: (0, 0))],
        out_specs=pl.BlockSpec((tm, D), lambda i: (i, 0)),
        out_shape=jax.ShapeDtypeStruct((T, D), out_dtype),
        compiler_params=_params("parallel"),
        name="add_rmsnorm_rows",
    )(x2d, y2d, gain.reshape(1, D).astype(F32))


def _mm_body(a_ref, b_ref, o_ref, *, scale):
    acc = jnp.dot(a_ref[...], b_ref[...], preferred_element_type=F32)
    if scale != 1.0:
        acc = acc * scale
    o_ref[...] = acc.astype(o_ref.dtype)


def _mm_res_body(a_ref, b_ref, r_ref, o_ref):
    acc = jnp.dot(a_ref[...], b_ref[...], preferred_element_type=F32)
    o_ref[...] = (r_ref[...].astype(F32) + acc).astype(o_ref.dtype)


def matmul(a, b, out_dtype, *, residual=None, scale=1.0, tm=1024, tn=512, name="matmul"):
    M = a.shape[0]
    K, N = b.shape
    tm = _tile(M, tm)
    tn = _tile(N, tn)
    in_specs = [pl.BlockSpec((tm, K), lambda i, j: (i, 0)),
                pl.BlockSpec((K, tn), lambda i, j: (0, j))]
    args = [a, b]
    if residual is None:
        body = functools.partial(_mm_body, scale=scale)
    else:
        body = _mm_res_body
        in_specs.append(pl.BlockSpec((tm, tn), lambda i, j: (i, j)))
        args.append(residual)
    return pl.pallas_call(
        body,
        grid=(M // tm, N // tn),
        in_specs=in_specs,
        out_specs=pl.BlockSpec((tm, tn), lambda i, j: (i, j)),
        out_shape=jax.ShapeDtypeStruct((M, N), out_dtype),
        compiler_params=_params("parallel", "arbitrary"),
        name=name,
    )(*args)


def _norm_mm_body(x_ref, g_ref, b_ref, o_ref, h_scr, *, scale):
    @pl.when(pl.program_id(1) == 0)
    def _():
        x = x_ref[...].astype(F32)
        ms = jnp.mean(x * x, axis=-1, keepdims=True)
        h_scr[...] = (x * lax.rsqrt(ms + EPS) * g_ref[...]).astype(h_scr.dtype)

    acc = jnp.dot(h_scr[...], b_ref[...], preferred_element_type=F32)
    if scale != 1.0:
        acc = acc * scale
    o_ref[...] = acc.astype(o_ref.dtype)


def norm_matmul(x2d, gain, b, out_dtype, *, scale=1.0, tm=512, tn=512, name="norm_matmul"):
    M, K = x2d.shape
    N = b.shape[1]
    tm = _tile(M, tm)
    tn = _tile(N, tn)
    return pl.pallas_call(
        functools.partial(_norm_mm_body, scale=scale),
        grid=(M // tm, N // tn),
        in_specs=[pl.BlockSpec((tm, K), lambda i, j: (i, 0)),
                  pl.BlockSpec((1, K), lambda i, j: (0, 0)),
                  pl.BlockSpec((K, tn), lambda i, j: (0, j))],
        out_specs=pl.BlockSpec((tm, tn), lambda i, j: (i, j)),
        out_shape=jax.ShapeDtypeStruct((M, N), out_dtype),
        scratch_shapes=[pltpu.VMEM((tm, K), BF16)],
        compiler_params=_params("parallel", "arbitrary"),
        name=name,
    )(x2d, gain.reshape(1, K).astype(F32), b)


def _qk_prep_body(x_ref, g_ref, cos_ref, sin_ref, o_ref, *, heads):
    cos = cos_ref[...]
    sin = sin_ref[...]
    lane = lax.broadcasted_iota(jnp.int32, cos.shape, 1)
    first_half = (lane // (HEAD_DIM // 4)) % 2 == 0
    for hh in range(heads):
        sl = slice(hh * HEAD_DIM, (hh + 1) * HEAD_DIM)
        x = x_ref[:, sl].astype(F32)
        ms = jnp.mean(x * x, axis=-1, keepdims=True)
        y = x * lax.rsqrt(ms + EPS) * g_ref[:, sl]
        up = pltpu.roll(y, HEAD_DIM - HEAD_DIM // 4, 1)
        down = pltpu.roll(y, HEAD_DIM // 4, 1)
        partner = jnp.where(first_half, up, down)
        o_ref[:, sl] = (y * cos + partner * sin).astype(o_ref.dtype)


def qk_prep(proj, col0, gains, cos, sin, seq_len, tm=512):
    T = proj.shape[0]
    width = gains.shape[1]
    tm = _tile(seq_len, tm)
    heads = next(n for n in (4, 2, 1) if col0 % (n * HEAD_DIM) == 0 and width % (n * HEAD_DIM) == 0)
    bw = heads * HEAD_DIM
    cb0 = col0 // bw
    spb = seq_len // tm
    return pl.pallas_call(
        functools.partial(_qk_prep_body, heads=heads),
        grid=(T // tm, width // bw),
        in_specs=[pl.BlockSpec((tm, bw), lambda i, j: (i, cb0 + j)),
                  pl.BlockSpec((1, bw), lambda i, j: (0, j)),
                  pl.BlockSpec((tm, HEAD_DIM), lambda i, j: (i % spb, 0)),
                  pl.BlockSpec((tm, HEAD_DIM), lambda i, j: (i % spb, 0))],
        out_specs=pl.BlockSpec((tm, bw), lambda i, j: (i, j)),
        out_shape=jax.ShapeDtypeStruct((T, width), BF16),
        compiler_params=_params("parallel", "arbitrary"),
        name="qk_prep",
    )(proj, gains, cos, sin)


def rope_tables(seq_len):
    half = HEAD_DIM // 2
    t = jnp.arange(seq_len)
    pos = jnp.stack([t // GRID_W, t % GRID_W], axis=-1).astype(F32)
    inv_freq = ROPE_THETA ** (-jnp.arange(0, half, 2, dtype=F32) / half)
    ang = pos[:, :, None] * inv_freq
    cos = jnp.cos(ang)
    sin = jnp.sin(ang)
    cos_full = jnp.stack([cos, cos], axis=2).reshape(seq_len, HEAD_DIM)
    sin_full = jnp.stack([-sin, sin], axis=2).reshape(seq_len, HEAD_DIM)
    return cos_full, sin_full


def _attn_body(q_ref, k_ref, v_ref, o_ref):
    k = k_ref[...]
    v = v_ref[...]
    for g in range(GQA_GROUP):
        sl = slice(g * HEAD_DIM, (g + 1) * HEAD_DIM)
        s = lax.dot_general(q_ref[:, sl], k, (((1,), (1,)), ((), ())),
                            preferred_element_type=F32)
        m = jnp.max(s, axis=-1, keepdims=True)
        p = jnp.exp(s - m)
        l = jnp.sum(p, axis=-1, keepdims=True)
        o = jnp.dot(p.astype(BF16), v, preferred_element_type=F32)
        o_ref[:, sl] = (o / l).astype(o_ref.dtype)


def gqa_attention(qk, proj, v_col0, B, S, tq=256):
    tq = _tile(S, tq)
    qk3 = qk.reshape(B, S, qk.shape[1])
    proj3 = proj.reshape(B, S, proj.shape[1])
    gw = GQA_GROUP * HEAD_DIM
    kb0 = (N_HEADS * HEAD_DIM) // HEAD_DIM
    vb0 = v_col0 // HEAD_DIM
    out = pl.pallas_call(
        _attn_body,
        grid=(B, N_KV_HEADS, S // tq),
        in_specs=[pl.BlockSpec((None, tq, gw), lambda b, h, i: (b, i, h)),
                  pl.BlockSpec((None, S, HEAD_DIM), lambda b, h, i: (b, 0, kb0 + h)),
                  pl.BlockSpec((None, S, HEAD_DIM), lambda b, h, i: (b, 0, vb0 + h))],
        out_specs=pl.BlockSpec((None, tq, gw), lambda b, h, i: (b, i, h)),
        out_shape=jax.ShapeDtypeStruct((B, S, N_HEADS * HEAD_DIM), BF16),
        compiler_params=_params("parallel", "parallel", "arbitrary"),
        name="gqa_attention",
    )(qk3, qk3, proj3)
    return out.reshape(B * S, N_HEADS * HEAD_DIM)


def dft_tables(n, dtype=BF16):
    r = 64 if n % 64 == 0 else 1
    k = jnp.arange(n, dtype=jnp.int32)[None, :]

    def thin(rows):
        ang = ((rows[:, None] * k) % n).astype(F32) * (2.0 * math.pi / n)
        return jnp.cos(ang), jnp.sin(ang)

    c_hi, s_hi = thin(jnp.arange(n // r, dtype=jnp.int32) * r)
    c_lo, s_lo = thin(jnp.arange(r, dtype=jnp.int32))
    cos = c_hi[:, None, :] * c_lo[None, :, :] - s_hi[:, None, :] * s_lo[None, :, :]
    sin = s_hi[:, None, :] * c_lo[None, :, :] + c_hi[:, None, :] * s_lo[None, :, :]
    return cos.reshape(n, n).astype(dtype), sin.reshape(n, n).astype(dtype)


def _dft2_body(cs_ref, ss_ref, pc_ref, ps_ref, o_ref, *, scale):
    acc = jnp.dot(cs_ref[...], pc_ref[...], preferred_element_type=F32)
    acc = acc - jnp.dot(ss_ref[...], ps_ref[...], preferred_element_type=F32)
    o_ref[...] = (acc * scale).astype(o_ref.dtype)


def dft_positions(cs, ss, p, B, S, C, scale, tm=512, tn=512):
    tm = _tile(S, tm)
    tn = _tile(C, tn)
    p3 = p.reshape(B, S, 2 * C)
    nj = C // tn
    out = pl.pallas_call(
        functools.partial(_dft2_body, scale=scale),
        grid=(S // tm, B, nj),
        in_specs=[pl.BlockSpec((tm, S), lambda i, b, j: (i, 0)),
                  pl.BlockSpec((tm, S), lambda i, b, j: (i, 0)),
                  pl.BlockSpec((None, S, tn), lambda i, b, j: (b, 0, j)),
                  pl.BlockSpec((None, S, tn), lambda i, b, j: (b, 0, nj + j))],
        out_specs=pl.BlockSpec((None, tm, tn), lambda i, b, j: (b, i, j)),
        out_shape=jax.ShapeDtypeStruct((B, S, C), BF16),
        compiler_params=_params("parallel", "arbitrary", "arbitrary"),
        name="dft_positions",
    )(cs, ss, p3, p3)
    return out.reshape(B * S, C)


def _merge_body(scale_ref, o_ref, f_ref, h_ref, wa_ref, wf_ref, wg0_ref, wg1_ref, b0_ref, b1_ref, out_ref,
                h8_scr):
    @pl.when(pl.program_id(1) == 0)
    def _():
        h8_scr[...] = (h_ref[...].astype(F32) * scale_ref[0]).astype(F8)

    h8 = h8_scr[...]
    a_br = jnp.dot(o_ref[...], wa_ref[...], preferred_element_type=F32)
    f_br = jnp.dot(f_ref[...], wf_ref[...], preferred_element_type=F32)
    z0 = jnp.dot(h8, wg0_ref[...], preferred_element_type=F32) * scale_ref[1] + b0_ref[...]
    z1 = jnp.dot(h8, wg1_ref[...], preferred_element_type=F32) * scale_ref[1] + b1_ref[...]
    out_ref[...] = (jax.nn.sigmoid(z0) * a_br + jax.nn.sigmoid(z1) * f_br).astype(out_ref.dtype)


def branch_merge(o, fm, h, wa, wf, wg, bg, scales, tm=512, tn=512):
    T, D = h.shape
    tm = _tile(T, tm)
    tn = _tile(D, tn)
    nj = D // tn
    ka = o.shape[1]
    kf = fm.shape[1]
    bg2 = bg.reshape(1, 2 * D).astype(F32)
    return pl.pallas_call(
        _merge_body,
        grid=(T // tm, nj),
        in_specs=[pl.BlockSpec(memory_space=pltpu.SMEM),
                  pl.BlockSpec((tm, ka), lambda i, j: (i, 0)),
                  pl.BlockSpec((tm, kf), lambda i, j: (i, 0)),
                  pl.BlockSpec((tm, D), lambda i, j: (i, 0)),
                  pl.BlockSpec((ka, tn), lambda i, j: (0, j)),
                  pl.BlockSpec((kf, tn), lambda i, j: (0, j)),
                  pl.BlockSpec((D, tn), lambda i, j: (0, j)),
                  pl.BlockSpec((D, tn), lambda i, j: (0, nj + j)),
                  pl.BlockSpec((1, tn), lambda i, j: (0, j)),
                  pl.BlockSpec((1, tn), lambda i, j: (0, nj + j))],
        out_specs=pl.BlockSpec((tm, tn), lambda i, j: (i, j)),
        out_shape=jax.ShapeDtypeStruct((T, D), BF16),
        scratch_shapes=[pltpu.VMEM((tm, D), F8)],
        compiler_params=_params("parallel", "arbitrary"),
        name="branch_merge",
    )(scales, o, fm, h, wa, wf, wg, wg, bg2, bg2)


def _cross_attn_body(q_ref, kv_ref, o_ref):
    w = CA_HEADS * CA_HEAD_DIM
    for hh in range(CA_HEADS):
        sl = slice(hh * CA_HEAD_DIM, (hh + 1) * CA_HEAD_DIM)
        k = kv_ref[:, sl]
        v = kv_ref[:, w + hh * CA_HEAD_DIM: w + (hh + 1) * CA_HEAD_DIM]
        s = lax.dot_general(q_ref[:, sl], k, (((1,), (1,)), ((), ())),
                            preferred_element_type=F32)
        m = jnp.max(s, axis=-1, keepdims=True)
        p = jnp.exp(s - m)
        l = jnp.sum(p, axis=-1, keepdims=True)
        o = jnp.dot(p.astype(BF16), v, preferred_element_type=F32)
        o_ref[:, sl] = (o / l).astype(o_ref.dtype)


def cross_attention(qc, kv, B, S, M, tq=512):
    tq = _tile(S, tq)
    w = CA_HEADS * CA_HEAD_DIM
    out = pl.pallas_call(
        _cross_attn_body,
        grid=(B, S // tq),
        in_specs=[pl.BlockSpec((None, tq, w), lambda b, i: (b, i, 0)),
                  pl.BlockSpec((None, M, 2 * w), lambda b, i: (b, 0, 0))],
        out_specs=pl.BlockSpec((None, tq, w), lambda b, i: (b, i, 0)),
        out_shape=jax.ShapeDtypeStruct((B, S, w), BF16),
        compiler_params=_params("parallel", "arbitrary"),
        name="cross_attention",
    )(qc.reshape(B, S, w), kv.reshape(B, M, 2 * w))
    return out.reshape(B * S, w)


SUBLANES = 8


def _sort_network(n):
    pairs = []

    def merge(lo, length, r):
        step = r * 2
        if step < length:
            merge(lo, length, step)
            merge(lo + r, length, step)
            pairs.extend((i, i + r) for i in range(lo + r, lo + length - r, step))
        else:
            pairs.append((lo, lo + r))

    def sort(lo, length):
        if length > 1:
            half = length // 2
            sort(lo, half)
            sort(lo + half, half)
            merge(lo, length, 1)

    sort(0, 16)
    return [(i, j) for i, j in pairs if j < n]


def _compare_exchange(v, i, j):
    v[i], v[j] = jnp.maximum(v[i], v[j]), jnp.minimum(v[i], v[j])


def _bitonic_sort16(v):
    stride = 8
    while stride:
        for i in range(16):
            if not i & stride:
                _compare_exchange(v, i, i + stride)
        stride //= 2


def _merge_sublanes(v, shift, n_valid=16):
    def other(i):
        return pltpu.roll(v[i], shift, 0)

    out = []
    for i in range(16):
        j = 15 - i
        if i < n_valid and j < n_valid:
            out.append(jnp.maximum(v[i], other(j)))
        elif i < n_valid:
            out.append(v[i])
        else:
            out.append(other(j))
    return out


def _top16_sorted(s):
    v = [s[r * SUBLANES:(r + 1) * SUBLANES, :] for r in range(16)]
    for i, j in _sort_network(16):
        _compare_exchange(v, i, j)
    for shift in (4, 2, 1):
        v = _merge_sublanes(v, shift)
        _bitonic_sort16(v)
    return v


def _pack_sublanes(vals):
    row = lax.broadcasted_iota(jnp.int32, vals[0].shape, 0)
    out = vals[SUBLANES - 1]
    for r in range(SUBLANES - 2, -1, -1):
        out = jnp.where(row == r, vals[r], out)
    return out


def _pair_threshold(t1, t2):
    p_lo = _pack_sublanes(t2[:SUBLANES])
    p_hi = _pack_sublanes(t2[SUBLANES:])
    q_hi = _pack_sublanes(t1[SUBLANES:])
    cand = [t1[0] + p_lo, t1[0] + p_hi] + [t1[a] + p_lo for a in range(1, SUBLANES)] + [q_hi + t2[0]]
    v = list(cand)
    n = len(v)
    for i, j in _sort_network(n):
        _compare_exchange(v, i, j)
    v = _merge_sublanes(v, 4, n_valid=n)
    _bitonic_sort16(v)
    v = _merge_sublanes(v, 2)
    _bitonic_sort16(v)
    v = _merge_sublanes(v, 1)
    tau = functools.reduce(jnp.minimum, v)
    return tau, cand


def _peer_route_body(hscale_ref, x_ref, g_ref, wq_ref, keys_ref,
                     s1_ref, s2_ref, e1_ref, e2_ref, tau_ref, h8_ref, q_scr, *, tl):
    x = x_ref[...]
    ms = jnp.mean(x * x, axis=-1, keepdims=True)
    h = (x * lax.rsqrt(ms + EPS) * g_ref[...]).astype(BF16)
    h8_ref[...] = (h.astype(F32) * hscale_ref[0]).astype(F8)
    q_scr[...] = lax.dot_general(wq_ref[...], h, (((1,), (1,)), ((), ())),
                                 preferred_element_type=F32)
    tm = x_ref.shape[0]
    k1 = keys_ref[0]
    k2 = keys_ref[1]

    def per_head(hh, carry):
        r0 = pl.multiple_of(hh * (2 * PEER_HALF), 2 * PEER_HALF)
        for lc in range(tm // tl):
            ls = slice(lc * tl, (lc + 1) * tl)
            s1 = jnp.dot(k1, q_scr[pl.ds(r0, PEER_HALF), ls], preferred_element_type=F32,
                         precision=lax.Precision.HIGHEST)
            s2 = jnp.dot(k2, q_scr[pl.ds(r0 + PEER_HALF, PEER_HALF), ls],
                         preferred_element_type=F32, precision=lax.Precision.HIGHEST)
            t1 = _top16_sorted(s1)
            t2 = _top16_sorted(s2)
            tau, cand = _pair_threshold(t1, t2)
            m1 = t1[0][:1]
            m2 = t2[0][:1]
            top = t1[0] + t2[0]
            zs = [jnp.where(c >= tau, jnp.exp(c - top), 0.0) for c in cand]
            z = jnp.sum(functools.reduce(lambda x, y: x + y, zs), axis=0, keepdims=True)
            s1_ref[hh, :, ls] = s1
            s2_ref[hh, :, ls] = s2
            e1_ref[hh, :, ls] = jnp.exp(s1 - m1)
            e2_ref[hh, :, ls] = jnp.exp(s2 - m2) / z
            tau_ref[hh, :, ls] = tau[:1]
        return carry

    lax.fori_loop(0, PEER_HEADS, per_head, 0)


def peer_route(x2d, gain, h_scale, wq_t, keys, tm=256, tl=128):
    T, D = x2d.shape
    tm = _tile(T, tm)
    tl = _tile(tm, tl)
    qw = wq_t.shape[0]
    big = jax.ShapeDtypeStruct((PEER_HEADS, N_KEYS, T), F32)
    big_spec = pl.BlockSpec((PEER_HEADS, N_KEYS, tm), lambda i: (0, 0, i))
    return pl.pallas_call(
        functools.partial(_peer_route_body, tl=tl),
        grid=(T // tm,),
        in_specs=[pl.BlockSpec(memory_space=pltpu.SMEM),
                  pl.BlockSpec((tm, D), lambda i: (i, 0)),
                  pl.BlockSpec((1, D), lambda i: (0, 0)),
                  pl.BlockSpec((qw, D), lambda i: (0, 0)),
                  pl.BlockSpec((2, N_KEYS, PEER_HALF), lambda i: (0, 0, 0))],
        out_specs=[big_spec, big_spec, big_spec, big_spec,
                   pl.BlockSpec((PEER_HEADS, 1, tm), lambda i: (0, 0, i)),
                   pl.BlockSpec((tm, D), lambda i: (i, 0))],
        out_shape=[big, big, big, big, jax.ShapeDtypeStruct((PEER_HEADS, 1, T), F32),
                   jax.ShapeDtypeStruct((T, D), F8)],
        scratch_shapes=[pltpu.VMEM((qw, tm), F32)],
        compiler_params=_params("parallel"),
        name="peer_route",
    )(h_scale.reshape(1).astype(F32), x2d, gain.reshape(1, D).astype(F32), wq_t, keys)


def _peer_expert_body(scale_ref, h_ref, u_ref, v_ref, s1_ref, s2_ref, e1_ref, e2_ref, tau_ref, o_ref,
                      a0, a1, *, nc, nk, n_chunks):
    s = pl.program_id(0)

    @pl.when(s == 0)
    def _():
        for ref in (a0, a1):
            ref[...] = jnp.zeros_like(ref)

    @pl.when((s == 0) | ((s >= 1) & ((s - 1) % nk == 0)))
    def _():
        o_ref[...] = jnp.zeros_like(o_ref)

    ec, tm = a0.shape
    d_model = o_ref.shape[1]
    chunk = jnp.clip(s - 1, 0, n_chunks - 1) % nk
    tok_w = min(tm, 2 * LANES)
    dcol_w = min(d_model, 4 * LANES)
    inv_u_scale = scale_ref[0]

    def project_piece(a_wr, tc):
        cols = slice(tc * tok_w, (tc + 1) * tok_w)
        a_wr[:, cols] = lax.dot_general(u_ref[...], h_ref[cols, :], (((1,), (1,)), ((), ())),
                                        preferred_element_type=F32)

    def gate_piece(a_rd, cc, tl):
        c = chunk * nc + cc
        rows = slice(cc * N_KEYS, (cc + 1) * N_KEYS)
        cols = slice(tl * LANES, (tl + 1) * LANES)
        g = None
        for hh in range(PEER_HEADS):
            s1c = s1_ref[hh, pl.ds(c, 1), :][:, cols]
            e1c = e1_ref[hh, pl.ds(c, 1), :][:, cols]
            mask = (s2_ref[hh, :, cols] + s1c) >= tau_ref[hh, :, cols]
            term = jnp.where(mask, e2_ref[hh, :, cols] * e1c, 0.0)
            g = term if g is None else g + term
        a = a_rd[rows, cols] * inv_u_scale
        act = (0.5 * PEER_W_SCALE) * a * (1.0 + lax.erf(a * INV_SQRT2))
        wv = lax.clamp(-F8_MAX, act * g, F8_MAX)
        return wv.T.astype(F8)

    def apply_piece(half, w_half, dc):
        tok = slice(half * tok_w, (half + 1) * tok_w)
        cols = slice(dc * dcol_w, (dc + 1) * dcol_w)
        o_ref[tok, cols] += jnp.dot(w_half, v_ref[:, cols], preferred_element_type=F32)

    def step(a_wr, a_rd):
        lanes_per_half = tok_w // LANES
        for tc in range(tm // tok_w):
            project_piece(a_wr, tc)
        for half in range(tm // tok_w):
            tiles = [[gate_piece(a_rd, cc, half * lanes_per_half + tl) for cc in range(nc)]
                     for tl in range(lanes_per_half)]
            w_half = jnp.concatenate([jnp.concatenate(row, axis=1) for row in tiles], axis=0)
            for dc in range(d_model // dcol_w):
                apply_piece(half, w_half, dc)

    @pl.when(s % 2 == 0)
    def _():
        step(a0, a1)

    @pl.when(s % 2 == 1)
    def _():
        step(a1, a0)

    @pl.when((s >= 1) & ((s - 1) % nk == nk - 1))
    def _():
        o_ref[...] = o_ref[...] * scale_ref[1]


def _pow2_scale(x, target):
    m = jnp.max(jnp.abs(x)).astype(F32)
    p = jnp.exp2(jnp.floor(jnp.log2(target / jnp.maximum(m, jnp.finfo(F32).tiny))))
    return jnp.where(m > 0, jnp.clip(p, 2.0 ** -60, 2.0 ** 60), 1.0)


def peer_experts(hf8, u8, v8, scales, s1, s2, e1, e2, tau, tm=512, ec=512):
    T, D = hf8.shape
    NE = u8.shape[0]
    tm = _tile(T, tm)
    ec = _tile(NE, ec)
    nc = ec // N_KEYS
    nk = NE // ec
    n_chunks = (T // tm) * nk
    last = n_chunks - 1

    def lag(s, d):
        return jnp.clip(s - d, 0, last)

    big_spec = pl.BlockSpec((PEER_HEADS, N_KEYS, tm), lambda s: (0, 0, lag(s, 1) // nk))
    return pl.pallas_call(
        functools.partial(_peer_expert_body, nc=nc, nk=nk, n_chunks=n_chunks),
        grid=(n_chunks + 1,),
        in_specs=[pl.BlockSpec(memory_space=pltpu.SMEM),
                  pl.BlockSpec((tm, D), lambda s: (lag(s, 0) // nk, 0)),
                  pl.BlockSpec((ec, D), lambda s: (lag(s, 0) % nk, 0)),
                  pl.BlockSpec((ec, D), lambda s: (lag(s, 1) % nk, 0)),
                  big_spec, big_spec, big_spec, big_spec,
                  pl.BlockSpec((PEER_HEADS, 1, tm), lambda s: (0, 0, lag(s, 1) // nk))],
        out_specs=pl.BlockSpec((tm, D), lambda s: (lag(s, 1) // nk, 0)),
        out_shape=jax.ShapeDtypeStruct((T, D), F32),
        scratch_shapes=[pltpu.VMEM((ec, tm), F32), pltpu.VMEM((ec, tm), F32)],
        compiler_params=_params("arbitrary"),
        name="peer_experts",
    )(scales, hf8, u8, v8, s1, s2, e1, e2, tau)


def _trunk(x, mem, w):
    B, S, D = x.shape
    M = mem.shape[1]
    T = B * S
    C = w["four_cols"]
    x2d = x.reshape(T, D)

    h = rmsnorm_rows(x2d, w["norm_mix"], BF16)
    proj = matmul(h, w["w_in"], BF16, name="in_proj")

    cos, sin = rope_tables(S)
    qk = qk_prep(proj, C, w["qk_gain"], cos, sin, S)
    o = gqa_attention(qk, proj, C + (N_HEADS + N_KV_HEADS) * HEAD_DIM, B, S)

    cs, ss = w["dft_tables"][S] if S in w["dft_tables"] else dft_tables(S)
    p = matmul(proj, w["dft_ch"], BF16, name="dft_channels")
    fm = dft_positions(cs, ss, p, B, S, C, 1.0 / math.sqrt(S * C))

    merged = branch_merge(o, fm, h, w["w_attn_br"], w["w_four_br"], w["w_gate"], w["b_gate"],
                          w["gate_scales"])
    x1 = matmul(merged, w["w_out"], F32, residual=x2d, name="out_proj")

    qc = norm_matmul(x1, w["norm_ca"], w["w_cq"], BF16, scale=CA_HEAD_DIM ** -0.5, name="ca_q")
    kv = norm_matmul(mem.reshape(B * M, D), w["mem_norm"], w["w_ckv"], BF16, name="ca_kv")
    oc = cross_attention(qc, kv, B, S, M)
    x2 = matmul(oc, w["w_co"], F32, residual=x1, name="ca_out")

    s1, s2, e1, e2, tau, hf8 = peer_route(x2, w["norm_ffn"], w["peer_h_scale"], w["w_pq_t"], w["sub_keys"])
    po = peer_experts(hf8, w["expert_u"], w["expert_v"], w["peer_scales"], s1, s2, e1, e2, tau)
    y = add_rmsnorm_rows(x2, po, w["final_norm"], F32)
    return y.reshape(B, S, D)


def kernel(x_prompt, x_sample, mem_prompt, mem_sample, norm_mix, w_in, q_norm, k_norm, w_attn_br, w_four_br, w_gate, b_gate, w_out, norm_ca, mem_norm, w_cq, w_ckv, w_co, norm_ffn, w_pq, sub_keys, expert_u, expert_v, final_norm):
    assert norm_mix.shape[0] == 1, "single-layer trunk"
    C = w_four_br.shape[1]
    attn_w = N_HEADS * HEAD_DIM
    kv_w = N_KV_HEADS * HEAD_DIM
    wi = w_in[0]
    w_in_r = jnp.concatenate([wi[:, attn_w + 2 * kv_w:], wi[:, :attn_w + 2 * kv_w]], axis=1).astype(BF16)
    cc, sc = dft_tables(C)
    dft_ch = jnp.concatenate([cc, sc], axis=1)
    scale = HEAD_DIM ** -0.5
    qk_gain = jnp.concatenate([jnp.tile(q_norm[0] * scale, N_HEADS),
                               jnp.tile(k_norm[0], N_KV_HEADS)]).reshape(1, -1).astype(F32)
    g_scale = _pow2_scale(w_gate[0], F8_OPERAND_TARGET)
    hm_scale = _pow2_scale(norm_mix[0] * math.sqrt(norm_mix.shape[1]), F8_OPERAND_TARGET)
    u_scale = _pow2_scale(expert_u[0], F8_OPERAND_TARGET)
    v_scale = _pow2_scale(expert_v[0], F8_OPERAND_TARGET)
    h_scale = _pow2_scale(norm_ffn[0] * math.sqrt(norm_ffn.shape[1]), F8_OPERAND_TARGET)
    w = dict(
        four_cols=C,
        norm_mix=norm_mix[0], w_in=w_in_r, qk_gain=qk_gain, dft_ch=dft_ch, dft_tables={C: (cc, sc)},
        w_attn_br=w_attn_br[0].astype(BF16), w_four_br=w_four_br[0].astype(BF16),
        w_gate=(w_gate[0] * g_scale).astype(F8), b_gate=b_gate[0], w_out=w_out[0].astype(BF16),
        gate_scales=jnp.stack([hm_scale, 1.0 / (hm_scale * g_scale)]).astype(F32),
        norm_ca=norm_ca[0], mem_norm=mem_norm[0], w_cq=w_cq[0].astype(BF16),
        w_ckv=w_ckv[0].astype(BF16), w_co=w_co[0].astype(BF16), norm_ffn=norm_ffn[0],
        w_pq_t=w_pq[0].T.astype(BF16), sub_keys=sub_keys[0].astype(F32),
        expert_u=(expert_u[0] * u_scale).astype(F8), expert_v=(expert_v[0] * v_scale).astype(F8),
        peer_h_scale=h_scale,
        peer_scales=jnp.stack([1.0 / (u_scale * h_scale), 1.0 / (PEER_W_SCALE * v_scale)]).astype(F32),
        final_norm=final_norm,
    )
    y_prompt = _trunk(x_prompt, mem_prompt, w)
    y_sample = _trunk(x_sample, mem_sample, w)
    return (y_prompt, y_sample)
```

```python
import functools
import math

import jax
import jax.numpy as jnp
from jax import lax
from jax.experimental import pallas as pl
from jax.experimental.pallas import tpu as pltpu

F32 = jnp.float32
BF16 = jnp.bfloat16
F8 = jnp.float8_e4m3fn
F8_MAX = 448.0
F8_OPERAND_TARGET = 256.0
PEER_W_SCALE = 16.0

N_HEADS = 16
N_KV_HEADS = 4
HEAD_DIM = 128
GQA_GROUP = N_HEADS // N_KV_HEADS
ROPE_THETA = 10000.0
GRID_W = 64
CA_HEADS = 4
CA_HEAD_DIM = 256
N_KEYS = 128
PEER_HEADS = 8
PEER_TOPK = 16
PEER_HALF = 128
EPS = 1e-6
INV_SQRT2 = 0.7071067811865476

V7X_VMEM_BYTES = 64 * 1024 * 1024
VMEM_LIMIT = V7X_VMEM_BYTES - 8 * 1024 * 1024
LANES = 128


def _tile(dim, pref):
    t = min(pref, dim)
    while dim % t:
        t //= 2
    return t


def _params(*sem):
    return pltpu.CompilerParams(dimension_semantics=sem, vmem_limit_bytes=VMEM_LIMIT)


def _rmsnorm_body(x_ref, g_ref, o_ref):
    x = x_ref[...].astype(F32)
    ms = jnp.mean(x * x, axis=-1, keepdims=True)
    o_ref[...] = (x * lax.rsqrt(ms + EPS) * g_ref[...]).astype(o_ref.dtype)


def rmsnorm_rows(x2d, gain, out_dtype, tm=256):
    T, D = x2d.shape
    tm = _tile(T, tm)
    return pl.pallas_call(
        _rmsnorm_body,
        grid=(T // tm,),
        in_specs=[pl.BlockSpec((tm, D), lambda i: (i, 0)),
                  pl.BlockSpec((1, D), lambda i: (0, 0))],
        out_specs=pl.BlockSpec((tm, D), lambda i: (i, 0)),
        out_shape=jax.ShapeDtypeStruct((T, D), out_dtype),
        compiler_params=_params("parallel"),
        name="rmsnorm_rows",
    )(x2d, gain.reshape(1, D).astype(F32))


def _add_rmsnorm_body(x_ref, y_ref, g_ref, o_ref):
    x = x_ref[...].astype(F32) + y_ref[...].astype(F32)
    ms = jnp.mean(x * x, axis=-1, keepdims=True)
    o_ref[...] = (x * lax.rsqrt(ms + EPS) * g_ref[...]).astype(o_ref.dtype)


def add_rmsnorm_rows(x2d, y2d, gain, out_dtype, tm=256):
    T, D = x2d.shape
    tm = _tile(T, tm)
    return pl.pallas_call(
        _add_rmsnorm_body,
        grid=(T // tm,),
        in_specs=[pl.BlockSpec((tm, D), lambda i: (i, 0)),
                  pl.BlockSpec((tm, D), lambda i: (i, 0)),
                  pl.BlockSpec((1, D), lambda i: (0, 0))],
        out_specs=pl.BlockSpec((tm, D), lambda i: (i, 0)),
        out_shape=jax.ShapeDtypeStruct((T, D), out_dtype),
        compiler_params=_params("parallel"),
        name="add_rmsnorm_rows",
    )(x2d, y2d, gain.reshape(1, D).astype(F32))


def _mm_body(a_ref, b_ref, o_ref, *, scale):
    acc = jnp.dot(a_ref[...], b_ref[...], preferred_element_type=F32)
    if scale != 1.0:
        acc = acc * scale
    o_ref[...] = acc.astype(o_ref.dtype)


def _mm_res_body(a_ref, b_ref, r_ref, o_ref):
    acc = jnp.dot(a_ref[...], b_ref[...], preferred_element_type=F32)
    o_ref[...] = (r_ref[...].astype(F32) + acc).astype(o_ref.dtype)


def matmul(a, b, out_dtype, *, residual=None, scale=1.0, tm=1024, tn=512, name="matmul"):
    M = a.shape[0]
    K, N = b.shape
    tm = _tile(M, tm)
    tn = _tile(N, tn)
    in_specs = [pl.BlockSpec((tm, K), lambda i, j: (i, 0)),
                pl.BlockSpec((K, tn), lambda i, j: (0, j))]
    args = [a, b]
    if residual is None:
        body = functools.partial(_mm_body, scale=scale)
    else:
        body = _mm_res_body
        in_specs.append(pl.BlockSpec((tm, tn), lambda i, j: (i, j)))
        args.append(residual)
    return pl.pallas_call(
        body,
        grid=(M // tm, N // tn),
        in_specs=in_specs,
        out_specs=pl.BlockSpec((tm, tn), lambda i, j: (i, j)),
        out_shape=jax.ShapeDtypeStruct((M, N), out_dtype),
        compiler_params=_params("parallel", "arbitrary"),
        name=name,
    )(*args)


def _norm_mm_body(x_ref, g_ref, b_ref, o_ref, h_scr, *, scale):
    @pl.when(pl.program_id(1) == 0)
    def _():
        x = x_ref[...].astype(F32)
        ms = jnp.mean(x * x, axis=-1, keepdims=True)
        h_scr[...] = (x * lax.rsqrt(ms + EPS) * g_ref[...]).astype(h_scr.dtype)

    acc = jnp.dot(h_scr[...], b_ref[...], preferred_element_type=F32)
    if scale != 1.0:
        acc = acc * scale
    o_ref[...] = acc.astype(o_ref.dtype)


def norm_matmul(x2d, gain, b, out_dtype, *, scale=1.0, tm=512, tn=512, name="norm_matmul"):
    M, K = x2d.shape
    N = b.shape[1]
    tm = _tile(M, tm)
    tn = _tile(N, tn)
    return pl.pallas_call(
        functools.partial(_norm_mm_body, scale=scale),
        grid=(M // tm, N // tn),
        in_specs=[pl.BlockSpec((tm, K), lambda i, j: (i, 0)),
                  pl.BlockSpec((1, K), lambda i, j: (0, 0)),
                  pl.BlockSpec((K, tn), lambda i, j: (0, j))],
        out_specs=pl.BlockSpec((tm, tn), lambda i, j: (i, j)),
        out_shape=jax.ShapeDtypeStruct((M, N), out_dtype),
        scratch_shapes=[pltpu.VMEM((tm, K), BF16)],
        compiler_params=_params("parallel", "arbitrary"),
        name=name,
    )(x2d, gain.reshape(1, K).astype(F32), b)


def _head_norm_rope(x, gain, cos, sin):
    x = x.astype(F32)
    ms = jnp.mean(x * x, axis=-1, keepdims=True)
    y = x * lax.rsqrt(ms + EPS) * gain
    lane = lax.broadcasted_iota(jnp.int32, y.shape, 1)
    first_half = (lane // (HEAD_DIM // 4)) % 2 == 0
    up = pltpu.roll(y, HEAD_DIM - HEAD_DIM // 4, 1)
    down = pltpu.roll(y, HEAD_DIM // 4, 1)
    return y * cos + jnp.where(first_half, up, down) * sin


def _qk_prep_body(x_ref, g_ref, cos_ref, sin_ref, o_ref, *, heads):
    cos = cos_ref[...]
    sin = sin_ref[...]
    for hh in range(heads):
        sl = slice(hh * HEAD_DIM, (hh + 1) * HEAD_DIM)
        o_ref[:, sl] = _head_norm_rope(x_ref[:, sl], g_ref[:, sl], cos, sin).astype(o_ref.dtype)


def qk_prep(proj, col0, gains, cos, sin, seq_len, tm=512):
    T = proj.shape[0]
    width = gains.shape[1]
    tm = _tile(seq_len, tm)
    heads = next(n for n in (4, 2, 1) if col0 % (n * HEAD_DIM) == 0 and width % (n * HEAD_DIM) == 0)
    bw = heads * HEAD_DIM
    cb0 = col0 // bw
    spb = seq_len // tm
    return pl.pallas_call(
        functools.partial(_qk_prep_body, heads=heads),
        grid=(T // tm, width // bw),
        in_specs=[pl.BlockSpec((tm, bw), lambda i, j: (i, cb0 + j)),
                  pl.BlockSpec((1, bw), lambda i, j: (0, j)),
                  pl.BlockSpec((tm, HEAD_DIM), lambda i, j: (i % spb, 0)),
                  pl.BlockSpec((tm, HEAD_DIM), lambda i, j: (i % spb, 0))],
        out_specs=pl.BlockSpec((tm, bw), lambda i, j: (i, j)),
        out_shape=jax.ShapeDtypeStruct((T, width), BF16),
        compiler_params=_params("parallel", "arbitrary"),
        name="qk_prep",
    )(proj, gains, cos, sin)


def rope_tables(seq_len):
    half = HEAD_DIM // 2
    t = jnp.arange(seq_len)
    pos = jnp.stack([t // GRID_W, t % GRID_W], axis=-1).astype(F32)
    inv_freq = ROPE_THETA ** (-jnp.arange(0, half, 2, dtype=F32) / half)
    ang = pos[:, :, None] * inv_freq
    cos = jnp.cos(ang)
    sin = jnp.sin(ang)
    cos_full = jnp.stack([cos, cos], axis=2).reshape(seq_len, HEAD_DIM)
    sin_full = jnp.stack([-sin, sin], axis=2).reshape(seq_len, HEAD_DIM)
    return cos_full, sin_full


def _attn_body(q_ref, k_ref, v_ref, gq_ref, cos_ref, sin_ref, o_ref):
    k = k_ref[...]
    v = v_ref[...]
    cos = cos_ref[...]
    sin = sin_ref[...]
    for g in range(GQA_GROUP):
        sl = slice(g * HEAD_DIM, (g + 1) * HEAD_DIM)
        q = _head_norm_rope(q_ref[:, sl], gq_ref[...], cos, sin).astype(BF16)
        s = lax.dot_general(q, k, (((1,), (1,)), ((), ())),
                            preferred_element_type=F32)
        m = jnp.max(s, axis=-1, keepdims=True)
        p = jnp.exp(s - m)
        l = jnp.sum(p, axis=-1, keepdims=True)
        o = jnp.dot(p.astype(BF16), v, preferred_element_type=F32)
        o_ref[:, sl] = (o / l).astype(o_ref.dtype)


def gqa_attention(proj, kk, q_gain, cos, sin, q_col0, v_col0, B, S, tq=256):
    tq = _tile(S, tq)
    kk3 = kk.reshape(B, S, kk.shape[1])
    proj3 = proj.reshape(B, S, proj.shape[1])
    gw = GQA_GROUP * HEAD_DIM
    assert q_col0 % gw == 0 and v_col0 % HEAD_DIM == 0
    qb0 = q_col0 // gw
    vb0 = v_col0 // HEAD_DIM
    out = pl.pallas_call(
        _attn_body,
        grid=(B, N_KV_HEADS, S // tq),
        in_specs=[pl.BlockSpec((None, tq, gw), lambda b, h, i: (b, i, qb0 + h)),
                  pl.BlockSpec((None, S, HEAD_DIM), lambda b, h, i: (b, 0, h)),
                  pl.BlockSpec((None, S, HEAD_DIM), lambda b, h, i: (b, 0, vb0 + h)),
                  pl.BlockSpec((1, HEAD_DIM), lambda b, h, i: (0, 0)),
                  pl.BlockSpec((tq, HEAD_DIM), lambda b, h, i: (i, 0)),
                  pl.BlockSpec((tq, HEAD_DIM), lambda b, h, i: (i, 0))],
        out_specs=pl.BlockSpec((None, tq, gw), lambda b, h, i: (b, i, h)),
        out_shape=jax.ShapeDtypeStruct((B, S, N_HEADS * HEAD_DIM), BF16),
        compiler_params=_params("parallel", "parallel", "arbitrary"),
        name="gqa_attention",
    )(proj3, kk3, proj3, q_gain, cos, sin)
    return out.reshape(B * S, N_HEADS * HEAD_DIM)


def dft_tables(n, dtype=BF16):
    r = 64 if n % 64 == 0 else 1
    k = jnp.arange(n, dtype=jnp.int32)[None, :]

    def thin(rows):
        ang = ((rows[:, None] * k) % n).astype(F32) * (2.0 * math.pi / n)
        return jnp.cos(ang), jnp.sin(ang)

    c_hi, s_hi = thin(jnp.arange(n // r, dtype=jnp.int32) * r)
    c_lo, s_lo = thin(jnp.arange(r, dtype=jnp.int32))
    cos = c_hi[:, None, :] * c_lo[None, :, :] - s_hi[:, None, :] * s_lo[None, :, :]
    sin = s_hi[:, None, :] * c_lo[None, :, :] + c_hi[:, None, :] * s_lo[None, :, :]
    return cos.reshape(n, n).astype(dtype), sin.reshape(n, n).astype(dtype)


def _dft2_body(cs_ref, ss_ref, pc_ref, ps_ref, o_ref, *, scale):
    acc = jnp.dot(cs_ref[...], pc_ref[...], preferred_element_type=F32)
    acc = acc - jnp.dot(ss_ref[...], ps_ref[...], preferred_element_type=F32)
    o_ref[...] = (acc * scale).astype(o_ref.dtype)


def dft_positions(cs, ss, p, B, S, C, scale, tm=512, tn=512):
    tm = _tile(S, tm)
    tn = _tile(C, tn)
    p3 = p.reshape(B, S, 2 * C)
    nj = C // tn
    out = pl.pallas_call(
        functools.partial(_dft2_body, scale=scale),
        grid=(S // tm, B, nj),
        in_specs=[pl.BlockSpec((tm, S), lambda i, b, j: (i, 0)),
                  pl.BlockSpec((tm, S), lambda i, b, j: (i, 0)),
                  pl.BlockSpec((None, S, tn), lambda i, b, j: (b, 0, j)),
                  pl.BlockSpec((None, S, tn), lambda i, b, j: (b, 0, nj + j))],
        out_specs=pl.BlockSpec((None, tm, tn), lambda i, b, j: (b, i, j)),
        out_shape=jax.ShapeDtypeStruct((B, S, C), BF16),
        compiler_params=_params("parallel", "arbitrary", "arbitrary"),
        name="dft_positions",
    )(cs, ss, p3, p3)
    return out.reshape(B * S, C)


def _merge_body(scale_ref, o_ref, f_ref, h_ref, wa_ref, wf_ref, wg0_ref, wg1_ref, b0_ref, b1_ref, out_ref,
                h8_scr):
    @pl.when(pl.program_id(1) == 0)
    def _():
        h8_scr[...] = (h_ref[...].astype(F32) * scale_ref[0]).astype(F8)

    h8 = h8_scr[...]
    a_br = jnp.dot(o_ref[...], wa_ref[...], preferred_element_type=F32)
    f_br = jnp.dot(f_ref[...], wf_ref[...], preferred_element_type=F32)
    z0 = jnp.dot(h8, wg0_ref[...], preferred_element_type=F32) * scale_ref[1] + b0_ref[...]
    z1 = jnp.dot(h8, wg1_ref[...], preferred_element_type=F32) * scale_ref[1] + b1_ref[...]
    out_ref[...] = (jax.nn.sigmoid(z0) * a_br + jax.nn.sigmoid(z1) * f_br).astype(out_ref.dtype)


def branch_merge(o, fm, h, wa, wf, wg, bg, scales, tm=512, tn=512):
    T, D = h.shape
    tm = _tile(T, tm)
    tn = _tile(D, tn)
    nj = D // tn
    ka = o.shape[1]
    kf = fm.shape[1]
    bg2 = bg.reshape(1, 2 * D).astype(F32)
    return pl.pallas_call(
        _merge_body,
        grid=(T // tm, nj),
        in_specs=[pl.BlockSpec(memory_space=pltpu.SMEM),
                  pl.BlockSpec((tm, ka), lambda i, j: (i, 0)),
                  pl.BlockSpec((tm, kf), lambda i, j: (i, 0)),
                  pl.BlockSpec((tm, D), lambda i, j: (i, 0)),
                  pl.BlockSpec((ka, tn), lambda i, j: (0, j)),
                  pl.BlockSpec((kf, tn), lambda i, j: (0, j)),
                  pl.BlockSpec((D, tn), lambda i, j: (0, j)),
                  pl.BlockSpec((D, tn), lambda i, j: (0, nj + j)),
                  pl.BlockSpec((1, tn), lambda i, j: (0, j)),
                  pl.BlockSpec((1, tn), lambda i, j: (0, nj + j))],
        out_specs=pl.BlockSpec((tm, tn), lambda i, j: (i, j)),
        out_shape=jax.ShapeDtypeStruct((T, D), BF16),
        scratch_shapes=[pltpu.VMEM((tm, D), F8)],
        compiler_params=_params("parallel", "arbitrary"),
        name="branch_merge",
    )(scales, o, fm, h, wa, wf, wg, wg, bg2, bg2)


def _cross_attn_body(q_ref, kv_ref, o_ref):
    w = CA_HEADS * CA_HEAD_DIM
    for hh in range(CA_HEADS):
        sl = slice(hh * CA_HEAD_DIM, (hh + 1) * CA_HEAD_DIM)
        k = kv_ref[:, sl]
        v = kv_ref[:, w + hh * CA_HEAD_DIM: w + (hh + 1) * CA_HEAD_DIM]
        s = lax.dot_general(q_ref[:, sl], k, (((1,), (1,)), ((), ())),
                            preferred_element_type=F32)
        m = jnp.max(s, axis=-1, keepdims=True)
        p = jnp.exp(s - m)
        l = jnp.sum(p, axis=-1, keepdims=True)
        o = jnp.dot(p.astype(BF16), v, preferred_element_type=F32)
        o_ref[:, sl] = (o / l).astype(o_ref.dtype)


def cross_attention(qc, kv, B, S, M, tq=512):
    tq = _tile(S, tq)
    w = CA_HEADS * CA_HEAD_DIM
    out = pl.pallas_call(
        _cross_attn_body,
        grid=(B, S // tq),
        in_specs=[pl.BlockSpec((None, tq, w), lambda b, i: (b, i, 0)),
                  pl.BlockSpec((None, M, 2 * w), lambda b, i: (b, 0, 0))],
        out_specs=pl.BlockSpec((None, tq, w), lambda b, i: (b, i, 0)),
        out_shape=jax.ShapeDtypeStruct((B, S, w), BF16),
        compiler_params=_params("parallel", "arbitrary"),
        name="cross_attention",
    )(qc.reshape(B, S, w), kv.reshape(B, M, 2 * w))
    return out.reshape(B * S, w)


SUBLANES = 8


def _sort_network(n):
    pairs = []

    def merge(lo, length, r):
        step = r * 2
        if step < length:
            merge(lo, length, step)
            merge(lo + r, length, step)
            pairs.extend((i, i + r) for i in range(lo + r, lo + length - r, step))
        else:
            pairs.append((lo, lo + r))

    def sort(lo, length):
        if length > 1:
            half = length // 2
            sort(lo, half)
            sort(lo + half, half)
            merge(lo, length, 1)

    sort(0, 16)
    return [(i, j) for i, j in pairs if j < n]


def _compare_exchange(v, i, j):
    v[i], v[j] = jnp.maximum(v[i], v[j]), jnp.minimum(v[i], v[j])


def _bitonic_sort16(v):
    stride = 8
    while stride:
        for i in range(16):
            if not i & stride:
                _compare_exchange(v, i, i + stride)
        stride //= 2


def _merge_sublanes(v, shift, n_valid=16):
    def other(i):
        return pltpu.roll(v[i], shift, 0)

    out = []
    for i in range(16):
        j = 15 - i
        if i < n_valid and j < n_valid:
            out.append(jnp.maximum(v[i], other(j)))
        elif i < n_valid:
            out.append(v[i])
        else:
            out.append(other(j))
    return out


def _top16_sorted(s):
    v = [s[r * SUBLANES:(r + 1) * SUBLANES, :] for r in range(16)]
    for i, j in _sort_network(16):
        _compare_exchange(v, i, j)
    for shift in (4, 2, 1):
        v = _merge_sublanes(v, shift)
        _bitonic_sort16(v)
    return v


def _pack_sublanes(vals):
    row = lax.broadcasted_iota(jnp.int32, vals[0].shape, 0)
    out = vals[SUBLANES - 1]
    for r in range(SUBLANES - 2, -1, -1):
        out = jnp.where(row == r, vals[r], out)
    return out


def _pair_threshold(t1, t2):
    p_lo = _pack_sublanes(t2[:SUBLANES])
    p_hi = _pack_sublanes(t2[SUBLANES:])
    q_hi = _pack_sublanes(t1[SUBLANES:])
    cand = [t1[0] + p_lo, t1[0] + p_hi] + [t1[a] + p_lo for a in range(1, SUBLANES)] + [q_hi + t2[0]]
    v = list(cand)
    n = len(v)
    for i, j in _sort_network(n):
        _compare_exchange(v, i, j)
    v = _merge_sublanes(v, 4, n_valid=n)
    _bitonic_sort16(v)
    v = _merge_sublanes(v, 2)
    _bitonic_sort16(v)
    v = _merge_sublanes(v, 1)
    tau = functools.reduce(jnp.minimum, v)
    return tau, cand


def _peer_route_body(hscale_ref, x_ref, g_ref, wq_ref, keys_ref,
                     s1_ref, s2_ref, e1_ref, e2_ref, tau_ref, h8_ref, q_scr, *, tl):
    x = x_ref[...]
    ms = jnp.mean(x * x, axis=-1, keepdims=True)
    h = (x * lax.rsqrt(ms + EPS) * g_ref[...]).astype(BF16)
    h8_ref[...] = (h.astype(F32) * hscale_ref[0]).astype(F8)
    q_scr[...] = lax.dot_general(wq_ref[...], h, (((1,), (1,)), ((), ())),
                                 preferred_element_type=F32)
    tm = x_ref.shape[0]
    k1 = keys_ref[0]
    k2 = keys_ref[1]

    def per_head(hh, carry):
        r0 = pl.multiple_of(hh * (2 * PEER_HALF), 2 * PEER_HALF)
        for lc in range(tm // tl):
            ls = slice(lc * tl, (lc + 1) * tl)
            s1 = jnp.dot(k1, q_scr[pl.ds(r0, PEER_HALF), ls], preferred_element_type=F32,
                         precision=lax.Precision.HIGHEST)
            s2 = jnp.dot(k2, q_scr[pl.ds(r0 + PEER_HALF, PEER_HALF), ls],
                         preferred_element_type=F32, precision=lax.Precision.HIGHEST)
            t1 = _top16_sorted(s1)
            t2 = _top16_sorted(s2)
            tau, cand = _pair_threshold(t1, t2)
            m1 = t1[0][:1]
            m2 = t2[0][:1]
            top = t1[0] + t2[0]
            zs = [jnp.where(c >= tau, jnp.exp(c - top), 0.0) for c in cand]
            z = jnp.sum(functools.reduce(lambda x, y: x + y, zs), axis=0, keepdims=True)
            s1_ref[hh, :, ls] = s1
            s2_ref[hh, :, ls] = s2
            e1_ref[hh, :, ls] = jnp.exp(s1 - m1)
            e2_ref[hh, :, ls] = jnp.exp(s2 - m2) / z
            tau_ref[hh, :, ls] = tau[:1]
        return carry

    lax.fori_loop(0, PEER_HEADS, per_head, 0)


def peer_route(x2d, gain, h_scale, wq_t, keys, tm=256, tl=128):
    T, D = x2d.shape
    tm = _tile(T, tm)
    tl = _tile(tm, tl)
    qw = wq_t.shape[0]
    big = jax.ShapeDtypeStruct((PEER_HEADS, N_KEYS, T), F32)
    big_spec = pl.BlockSpec((PEER_HEADS, N_KEYS, tm), lambda i: (0, 0, i))
    return pl.pallas_call(
        functools.partial(_peer_route_body, tl=tl),
        grid=(T // tm,),
        in_specs=[pl.BlockSpec(memory_space=pltpu.SMEM),
                  pl.BlockSpec((tm, D), lambda i: (i, 0)),
                  pl.BlockSpec((1, D), lambda i: (0, 0)),
                  pl.BlockSpec((qw, D), lambda i: (0, 0)),
                  pl.BlockSpec((2, N_KEYS, PEER_HALF), lambda i: (0, 0, 0))],
        out_specs=[big_spec, big_spec, big_spec, big_spec,
                   pl.BlockSpec((PEER_HEADS, 1, tm), lambda i: (0, 0, i)),
                   pl.BlockSpec((tm, D), lambda i: (i, 0))],
        out_shape=[big, big, big, big, jax.ShapeDtypeStruct((PEER_HEADS, 1, T), F32),
                   jax.ShapeDtypeStruct((T, D), F8)],
        scratch_shapes=[pltpu.VMEM((qw, tm), F32)],
        compiler_params=_params("parallel"),
        name="peer_route",
    )(h_scale.reshape(1).astype(F32), x2d, gain.reshape(1, D).astype(F32), wq_t, keys)


def _peer_expert_body(scale_ref, h_ref, u_ref, v_ref, s1_ref, s2_ref, e1_ref, e2_ref, tau_ref, o_ref,
                      a0, a1, *, nc, nk, n_chunks):
    s = pl.program_id(0)

    @pl.when(s == 0)
    def _():
        for ref in (a0, a1):
            ref[...] = jnp.zeros_like(ref)

    @pl.when((s == 0) | ((s >= 1) & ((s - 1) % nk == 0)))
    def _():
        o_ref[...] = jnp.zeros_like(o_ref)

    ec, tm = a0.shape
    d_model = o_ref.shape[1]
    chunk = jnp.clip(s - 1, 0, n_chunks - 1) % nk
    tok_w = min(tm, 2 * LANES)
    dcol_w = min(d_model, 4 * LANES)
    act_scale = scale_ref[0] * (0.5 * PEER_W_SCALE)
    erf_scale = scale_ref[0] * INV_SQRT2

    def project_piece(a_wr, tc):
        cols = slice(tc * tok_w, (tc + 1) * tok_w)
        a_wr[:, cols] = lax.dot_general(u_ref[...], h_ref[cols, :], (((1,), (1,)), ((), ())),
                                        preferred_element_type=F32)

    def gate_piece(a_rd, cc, tl):
        c = chunk * nc + cc
        rows = slice(cc * N_KEYS, (cc + 1) * N_KEYS)
        cols = slice(tl * LANES, (tl + 1) * LANES)
        g = None
        for hh in range(PEER_HEADS):
            s1c = s1_ref[hh, pl.ds(c, 1), :][:, cols]
            e1c = e1_ref[hh, pl.ds(c, 1), :][:, cols]
            mask = (s2_ref[hh, :, cols] + s1c) >= tau_ref[hh, :, cols]
            term = jnp.where(mask, e2_ref[hh, :, cols] * e1c, 0.0)
            g = term if g is None else g + term
        a = a_rd[rows, cols]
        act = (a * act_scale) * (1.0 + lax.erf(a * erf_scale))
        wv = lax.clamp(-F8_MAX, act * g, F8_MAX)
        return wv.T.astype(F8)

    def apply_piece(half, w_half, dc):
        tok = slice(half * tok_w, (half + 1) * tok_w)
        cols = slice(dc * dcol_w, (dc + 1) * dcol_w)
        o_ref[tok, cols] += jnp.dot(w_half, v_ref[:, cols], preferred_element_type=F32)

    def step(a_wr, a_rd):
        lanes_per_half = tok_w // LANES
        for tc in range(tm // tok_w):
            project_piece(a_wr, tc)
        for half in range(tm // tok_w):
            tiles = [[gate_piece(a_rd, cc, half * lanes_per_half + tl) for cc in range(nc)]
                     for tl in range(lanes_per_half)]
            w_half = jnp.concatenate([jnp.concatenate(row, axis=1) for row in tiles], axis=0)
            for dc in range(d_model // dcol_w):
                apply_piece(half, w_half, dc)

    @pl.when(s % 2 == 0)
    def _():
        step(a0, a1)

    @pl.when(s % 2 == 1)
    def _():
        step(a1, a0)

    @pl.when((s >= 1) & ((s - 1) % nk == nk - 1))
    def _():
        o_ref[...] = o_ref[...] * scale_ref[1]


def _pow2_scale(x, target):
    m = jnp.max(jnp.abs(x)).astype(F32)
    p = jnp.exp2(jnp.floor(jnp.log2(target / jnp.maximum(m, jnp.finfo(F32).tiny))))
    return jnp.where(m > 0, jnp.clip(p, 2.0 ** -60, 2.0 ** 60), 1.0)


def peer_experts(hf8, u8, v8, scales, s1, s2, e1, e2, tau, tm=512, ec=1024):
    T, D = hf8.shape
    NE = u8.shape[0]
    tm = _tile(T, tm)
    ec = _tile(NE, ec)
    nc = ec // N_KEYS
    nk = NE // ec
    n_chunks = (T // tm) * nk
    last = n_chunks - 1

    def lag(s, d):
        return jnp.clip(s - d, 0, last)

    big_spec = pl.BlockSpec((PEER_HEADS, N_KEYS, tm), lambda s: (0, 0, lag(s, 1) // nk),
                            pipeline_mode=pl.Buffered(1))
    return pl.pallas_call(
        functools.partial(_peer_expert_body, nc=nc, nk=nk, n_chunks=n_chunks),
        grid=(n_chunks + 1,),
        in_specs=[pl.BlockSpec(memory_space=pltpu.SMEM),
                  pl.BlockSpec((tm, D), lambda s: (lag(s, 0) // nk, 0)),
                  pl.BlockSpec((ec, D), lambda s: (lag(s, 0) % nk, 0)),
                  pl.BlockSpec((ec, D), lambda s: (lag(s, 1) % nk, 0)),
                  big_spec, big_spec, big_spec, big_spec,
                  pl.BlockSpec((PEER_HEADS, 1, tm), lambda s: (0, 0, lag(s, 1) // nk))],
        out_specs=pl.BlockSpec((tm, D), lambda s: (lag(s, 1) // nk, 0)),
        out_shape=jax.ShapeDtypeStruct((T, D), F32),
        scratch_shapes=[pltpu.VMEM((ec, tm), F32), pltpu.VMEM((ec, tm), F32)],
        compiler_params=_params("arbitrary"),
        name="peer_experts",
    )(scales, hf8, u8, v8, s1, s2, e1, e2, tau)


def _trunk(x, mem, w):
    B, S, D = x.shape
    M = mem.shape[1]
    T = B * S
    C = w["four_cols"]
    x2d = x.reshape(T, D)

    h = rmsnorm_rows(x2d, w["norm_mix"], BF16)
    proj = matmul(h, w["w_in"], BF16, name="in_proj")

    cos, sin = rope_tables(S)
    k_col0 = C + N_HEADS * HEAD_DIM
    kk = qk_prep(proj, k_col0, w["k_gain"], cos, sin, S)
    o = gqa_attention(proj, kk, w["q_gain"], cos, sin, C, k_col0 + N_KV_HEADS * HEAD_DIM, B, S)

    cs, ss = w["dft_tables"][S] if S in w["dft_tables"] else dft_tables(S)
    p = matmul(proj, w["dft_ch"], BF16, name="dft_channels")
    fm = dft_positions(cs, ss, p, B, S, C, 1.0 / math.sqrt(S * C))

    merged = branch_merge(o, fm, h, w["w_attn_br"], w["w_four_br"], w["w_gate"], w["b_gate"],
                          w["gate_scales"])
    x1 = matmul(merged, w["w_out"], F32, residual=x2d, name="out_proj")

    hc = rmsnorm_rows(x1, w["norm_ca"], BF16)
    qc = matmul(hc, w["w_cq"], BF16, scale=CA_HEAD_DIM ** -0.5, name="ca_q")
    kv = norm_matmul(mem.reshape(B * M, D), w["mem_norm"], w["w_ckv"], BF16, name="ca_kv")
    oc = cross_attention(qc, kv, B, S, M)
    x2 = matmul(oc, w["w_co"], F32, residual=x1, name="ca_out")

    s1, s2, e1, e2, tau, hf8 = peer_route(x2, w["norm_ffn"], w["peer_h_scale"], w["w_pq_t"], w["sub_keys"])
    po = peer_experts(hf8, w["expert_u"], w["expert_v"], w["peer_scales"], s1, s2, e1, e2, tau)
    y = add_rmsnorm_rows(x2, po, w["final_norm"], F32)
    return y.reshape(B, S, D)


def kernel(x_prompt, x_sample, mem_prompt, mem_sample, norm_mix, w_in, q_norm, k_norm, w_attn_br, w_four_br, w_gate, b_gate, w_out, norm_ca, mem_norm, w_cq, w_ckv, w_co, norm_ffn, w_pq, sub_keys, expert_u, expert_v, final_norm):
    assert norm_mix.shape[0] == 1, "single-layer trunk"
    C = w_four_br.shape[1]
    attn_w = N_HEADS * HEAD_DIM
    kv_w = N_KV_HEADS * HEAD_DIM
    wi = w_in[0]
    w_in_r = jnp.concatenate([wi[:, attn_w + 2 * kv_w:], wi[:, :attn_w + 2 * kv_w]], axis=1).astype(BF16)
    cc, sc = dft_tables(C)
    dft_ch = jnp.concatenate([cc, sc], axis=1)
    scale = HEAD_DIM ** -0.5
    q_gain = (q_norm[0] * scale).reshape(1, HEAD_DIM).astype(F32)
    k_gain = jnp.tile(k_norm[0], N_KV_HEADS).reshape(1, -1).astype(F32)
    g_scale = _pow2_scale(w_gate[0], F8_OPERAND_TARGET)
    hm_scale = _pow2_scale(norm_mix[0] * math.sqrt(norm_mix.shape[1]), F8_OPERAND_TARGET)
    u_scale = _pow2_scale(expert_u[0], F8_OPERAND_TARGET)
    v_scale = _pow2_scale(expert_v[0], F8_OPERAND_TARGET)
    h_scale = _pow2_scale(norm_ffn[0] * math.sqrt(norm_ffn.shape[1]), F8_OPERAND_TARGET)
    w = dict(
        four_cols=C,
        norm_mix=norm_mix[0], w_in=w_in_r, q_gain=q_gain, k_gain=k_gain, dft_ch=dft_ch, dft_tables={C: (cc, sc)},
        w_attn_br=w_attn_br[0].astype(BF16), w_four_br=w_four_br[0].astype(BF16),
        w_gate=(w_gate[0] * g_scale).astype(F8), b_gate=b_gate[0], w_out=w_out[0].astype(BF16),
        gate_scales=jnp.stack([hm_scale, 1.0 / (hm_scale * g_scale)]).astype(F32),
        norm_ca=norm_ca[0], mem_norm=mem_norm[0], w_cq=w_cq[0].astype(BF16),
        w_ckv=w_ckv[0].astype(BF16), w_co=w_co[0].astype(BF16), norm_ffn=norm_ffn[0],
        w_pq_t=w_pq[0].T.astype(BF16), sub_keys=sub_keys[0].astype(F32),
        expert_u=(expert_u[0] * u_scale).astype(F8), expert_v=(expert_v[0] * v_scale).astype(F8),
        peer_h_scale=h_scale,
        peer_scales=jnp.stack([1.0 / (u_scale * h_scale), 1.0 / (PEER_W_SCALE * v_scale)]).astype(F32),
        final_norm=final_norm,
    )
    y_prompt = _trunk(x_prompt, mem_prompt, w)
    y_sample = _trunk(x_sample, mem_sample, w)
    return (y_prompt, y_sample)
```

```python
import functools
import math

import jax
import jax.numpy as jnp
from jax import lax
from jax.experimental import pallas as pl
from jax.experimental.pallas import tpu as pltpu

F32 = jnp.float32
BF16 = jnp.bfloat16
F8 = jnp.float8_e4m3fn
F8_MAX = 448.0
F8_OPERAND_TARGET = 256.0
PEER_W_SCALE = 16.0

N_HEADS = 16
N_KV_HEADS = 4
HEAD_DIM = 128
GQA_GROUP = N_HEADS // N_KV_HEADS
ROPE_THETA = 10000.0
GRID_W = 64
CA_HEADS = 4
CA_HEAD_DIM = 256
N_KEYS = 128
PEER_HEADS = 8
PEER_TOPK = 16
PEER_HALF = 128
EPS = 1e-6
INV_SQRT2 = 0.7071067811865476

V7X_VMEM_BYTES = 64 * 1024 * 1024
VMEM_LIMIT = V7X_VMEM_BYTES - 8 * 1024 * 1024
LANES = 128


def _tile(dim, pref):
    t = min(pref, dim)
    while dim % t:
        t //= 2
    return t


def _params(*sem):
    return pltpu.CompilerParams(dimension_semantics=sem, vmem_limit_bytes=VMEM_LIMIT)


def _rmsnorm_body(x_ref, g_ref, o_ref):
    x = x_ref[...].astype(F32)
    ms = jnp.mean(x * x, axis=-1, keepdims=True)
    o_ref[...] = (x * lax.rsqrt(ms + EPS) * g_ref[...]).astype(o_ref.dtype)


def rmsnorm_rows(x2d, gain, out_dtype, tm=256):
    T, D = x2d.shape
    tm = _tile(T, tm)
    return pl.pallas_call(
        _rmsnorm_body,
        grid=(T // tm,),
        in_specs=[pl.BlockSpec((tm, D), lambda i: (i, 0)),
                  pl.BlockSpec((1, D), lambda i: (0, 0))],
        out_specs=pl.BlockSpec((tm, D), lambda i: (i, 0)),
        out_shape=jax.ShapeDtypeStruct((T, D), out_dtype),
        compiler_params=_params("parallel"),
        name="rmsnorm_rows",
    )(x2d, gain.reshape(1, D).astype(F32))


def _add_rmsnorm_body(x_ref, y_ref, g_ref, o_ref):
    x = x_ref[...].astype(F32) + y_ref[...].astype(F32)
    ms = jnp.mean(x * x, axis=-1, keepdims=True)
    o_ref[...] = (x * lax.rsqrt(ms + EPS) * g_ref[...]).astype(o_ref.dtype)


def add_rmsnorm_rows(x2d, y2d, gain, out_dtype, tm=256):
    T, D = x2d.shape
    tm = _tile(T, tm)
    return pl.pallas_call(
        _add_rmsnorm_body,
        grid=(T // tm,),
        in_specs=[pl.BlockSpec((tm, D), lambda i: (i, 0)),
                  pl.BlockSpec((tm, D), lambda i: (i, 0)),
                  pl.BlockSpec((1, D), lambda i: (0, 0))],
        out_specs=pl.BlockSpec((tm, D), lambda i: (i, 0)),
        out_shape=jax.ShapeDtypeStruct((T, D), out_dtype),
        compiler_params=_params("parallel"),
        name="add_rmsnorm_rows",
    )(x2d, y2d, gain.reshape(1, D).astype(F32))


def _mm_body(a_ref, b_ref, o_ref, *, scale):
    acc = jnp.dot(a_ref[...], b_ref[...], preferred_element_type=F32)
    if scale != 1.0:
        acc = acc * scale
    o_ref[...] = acc.astype(o_ref.dtype)


def _mm_res_body(a_ref, b_ref, r_ref, o_ref):
    acc = jnp.dot(a_ref[...], b_ref[...], preferred_element_type=F32)
    o_ref[...] = (r_ref[...].astype(F32) + acc).astype(o_ref.dtype)


def matmul(a, b, out_dtype, *, residual=None, scale=1.0, tm=1024, tn=512, name="matmul"):
    M = a.shape[0]
    K, N = b.shape
    tm = _tile(M, tm)
    tn = _tile(N, tn)
    in_specs = [pl.BlockSpec((tm, K), lambda i, j: (i, 0)),
                pl.BlockSpec((K, tn), lambda i, j: (0, j))]
    args = [a, b]
    if residual is None:
        body = functools.partial(_mm_body, scale=scale)
    else:
        body = _mm_res_body
        in_specs.append(pl.BlockSpec((tm, tn), lambda i, j: (i, j)))
        args.append(residual)
    return pl.pallas_call(
        body,
        grid=(M // tm, N // tn),
        in_specs=in_specs,
        out_specs=pl.BlockSpec((tm, tn), lambda i, j: (i, j)),
        out_shape=jax.ShapeDtypeStruct((M, N), out_dtype),
        compiler_params=_params("parallel", "arbitrary"),
        name=name,
    )(*args)


def _norm_mm_body(x_ref, g_ref, b_ref, o_ref, h_scr, *, scale):
    @pl.when(pl.program_id(1) == 0)
    def _():
        x = x_ref[...].astype(F32)
        ms = jnp.mean(x * x, axis=-1, keepdims=True)
        h_scr[...] = (x * lax.rsqrt(ms + EPS) * g_ref[...]).astype(h_scr.dtype)

    acc = jnp.dot(h_scr[...], b_ref[...], preferred_element_type=F32)
    if scale != 1.0:
        acc = acc * scale
    o_ref[...] = acc.astype(o_ref.dtype)


def norm_matmul(x2d, gain, b, out_dtype, *, scale=1.0, tm=512, tn=512, name="norm_matmul"):
    M, K = x2d.shape
    N = b.shape[1]
    tm = _tile(M, tm)
    tn = _tile(N, tn)
    return pl.pallas_call(
        functools.partial(_norm_mm_body, scale=scale),
        grid=(M // tm, N // tn),
        in_specs=[pl.BlockSpec((tm, K), lambda i, j: (i, 0)),
                  pl.BlockSpec((1, K), lambda i, j: (0, 0)),
                  pl.BlockSpec((K, tn), lambda i, j: (0, j))],
        out_specs=pl.BlockSpec((tm, tn), lambda i, j: (i, j)),
        out_shape=jax.ShapeDtypeStruct((M, N), out_dtype),
        scratch_shapes=[pltpu.VMEM((tm, K), BF16)],
        compiler_params=_params("parallel", "arbitrary"),
        name=name,
    )(x2d, gain.reshape(1, K).astype(F32), b)


def _head_norm_rope(x, gain, cos, sin):
    x = x.astype(F32)
    ms = jnp.mean(x * x, axis=-1, keepdims=True)
    y = x * lax.rsqrt(ms + EPS) * gain
    lane = lax.broadcasted_iota(jnp.int32, y.shape, 1)
    first_half = (lane // (HEAD_DIM // 4)) % 2 == 0
    up = pltpu.roll(y, HEAD_DIM - HEAD_DIM // 4, 1)
    down = pltpu.roll(y, HEAD_DIM // 4, 1)
    return y * cos + jnp.where(first_half, up, down) * sin


def _qk_prep_body(x_ref, g_ref, cos_ref, sin_ref, o_ref, *, heads):
    cos = cos_ref[...]
    sin = sin_ref[...]
    for hh in range(heads):
        sl = slice(hh * HEAD_DIM, (hh + 1) * HEAD_DIM)
        o_ref[:, sl] = _head_norm_rope(x_ref[:, sl], g_ref[:, sl], cos, sin).astype(o_ref.dtype)


def qk_prep(proj, col0, gains, cos, sin, seq_len, tm=512):
    T = proj.shape[0]
    width = gains.shape[1]
    tm = _tile(seq_len, tm)
    heads = next(n for n in (4, 2, 1) if col0 % (n * HEAD_DIM) == 0 and width % (n * HEAD_DIM) == 0)
    bw = heads * HEAD_DIM
    cb0 = col0 // bw
    spb = seq_len // tm
    return pl.pallas_call(
        functools.partial(_qk_prep_body, heads=heads),
        grid=(T // tm, width // bw),
        in_specs=[pl.BlockSpec((tm, bw), lambda i, j: (i, cb0 + j)),
                  pl.BlockSpec((1, bw), lambda i, j: (0, j)),
                  pl.BlockSpec((tm, HEAD_DIM), lambda i, j: (i % spb, 0)),
                  pl.BlockSpec((tm, HEAD_DIM), lambda i, j: (i % spb, 0))],
        out_specs=pl.BlockSpec((tm, bw), lambda i, j: (i, j)),
        out_shape=jax.ShapeDtypeStruct((T, width), BF16),
        compiler_params=_params("parallel", "arbitrary"),
        name="qk_prep",
    )(proj, gains, cos, sin)


def rope_tables(seq_len):
    half = HEAD_DIM // 2
    t = jnp.arange(seq_len)
    pos = jnp.stack([t // GRID_W, t % GRID_W], axis=-1).astype(F32)
    inv_freq = ROPE_THETA ** (-jnp.arange(0, half, 2, dtype=F32) / half)
    ang = pos[:, :, None] * inv_freq
    cos = jnp.cos(ang)
    sin = jnp.sin(ang)
    cos_full = jnp.stack([cos, cos], axis=2).reshape(seq_len, HEAD_DIM)
    sin_full = jnp.stack([-sin, sin], axis=2).reshape(seq_len, HEAD_DIM)
    return cos_full, sin_full


def _attn_body(q_ref, k_ref, v_ref, gq_ref, cos_ref, sin_ref, o_ref):
    k = k_ref[...]
    v = v_ref[...]
    cos = cos_ref[...]
    sin = sin_ref[...]
    for g in range(GQA_GROUP):
        sl = slice(g * HEAD_DIM, (g + 1) * HEAD_DIM)
        q = _head_norm_rope(q_ref[:, sl], gq_ref[...], cos, sin).astype(BF16)
        s = lax.dot_general(q, k, (((1,), (1,)), ((), ())),
                            preferred_element_type=F32)
        m = jnp.max(s, axis=-1, keepdims=True)
        p = jnp.exp(s - m)
        l = jnp.sum(p, axis=-1, keepdims=True)
        o = jnp.dot(p.astype(BF16), v, preferred_element_type=F32)
        o_ref[:, sl] = (o / l).astype(o_ref.dtype)


def gqa_attention(proj, kk, q_gain, cos, sin, q_col0, v_col0, B, S, tq=256):
    tq = _tile(S, tq)
    kk3 = kk.reshape(B, S, kk.shape[1])
    proj3 = proj.reshape(B, S, proj.shape[1])
    gw = GQA_GROUP * HEAD_DIM
    assert q_col0 % gw == 0 and v_col0 % HEAD_DIM == 0
    qb0 = q_col0 // gw
    vb0 = v_col0 // HEAD_DIM
    out = pl.pallas_call(
        _attn_body,
        grid=(B, N_KV_HEADS, S // tq),
        in_specs=[pl.BlockSpec((None, tq, gw), lambda b, h, i: (b, i, qb0 + h)),
                  pl.BlockSpec((None, S, HEAD_DIM), lambda b, h, i: (b, 0, h)),
                  pl.BlockSpec((None, S, HEAD_DIM), lambda b, h, i: (b, 0, vb0 + h)),
                  pl.BlockSpec((1, HEAD_DIM), lambda b, h, i: (0, 0)),
                  pl.BlockSpec((tq, HEAD_DIM), lambda b, h, i: (i, 0)),
                  pl.BlockSpec((tq, HEAD_DIM), lambda b, h, i: (i, 0))],
        out_specs=pl.BlockSpec((None, tq, gw), lambda b, h, i: (b, i, h)),
        out_shape=jax.ShapeDtypeStruct((B, S, N_HEADS * HEAD_DIM), BF16),
        compiler_params=_params("parallel", "parallel", "arbitrary"),
        name="gqa_attention",
    )(proj3, kk3, proj3, q_gain, cos, sin)
    return out.reshape(B * S, N_HEADS * HEAD_DIM)


def dft_tables(n, dtype=BF16):
    r = 64 if n % 64 == 0 else 1
    k = jnp.arange(n, dtype=jnp.int32)[None, :]

    def thin(rows):
        ang = ((rows[:, None] * k) % n).astype(F32) * (2.0 * math.pi / n)
        return jnp.cos(ang), jnp.sin(ang)

    c_hi, s_hi = thin(jnp.arange(n // r, dtype=jnp.int32) * r)
    c_lo, s_lo = thin(jnp.arange(r, dtype=jnp.int32))
    cos = c_hi[:, None, :] * c_lo[None, :, :] - s_hi[:, None, :] * s_lo[None, :, :]
    sin = s_hi[:, None, :] * c_lo[None, :, :] + c_hi[:, None, :] * s_lo[None, :, :]
    return cos.reshape(n, n).astype(dtype), sin.reshape(n, n).astype(dtype)


def _dft2_body(cs_ref, ss_ref, pc_ref, ps_ref, o_ref, *, scale):
    acc = jnp.dot(cs_ref[...], pc_ref[...], preferred_element_type=F32)
    acc = acc - jnp.dot(ss_ref[...], ps_ref[...], preferred_element_type=F32)
    o_ref[...] = (acc * scale).astype(o_ref.dtype)


def dft_positions(cs, ss, p, B, S, C, scale, tm=512, tn=512):
    tm = _tile(S, tm)
    tn = _tile(C, tn)
    p3 = p.reshape(B, S, 2 * C)
    nj = C // tn
    out = pl.pallas_call(
        functools.partial(_dft2_body, scale=scale),
        grid=(S // tm, B, nj),
        in_specs=[pl.BlockSpec((tm, S), lambda i, b, j: (i, 0)),
                  pl.BlockSpec((tm, S), lambda i, b, j: (i, 0)),
                  pl.BlockSpec((None, S, tn), lambda i, b, j: (b, 0, j)),
                  pl.BlockSpec((None, S, tn), lambda i, b, j: (b, 0, nj + j))],
        out_specs=pl.BlockSpec((None, tm, tn), lambda i, b, j: (b, i, j)),
        out_shape=jax.ShapeDtypeStruct((B, S, C), BF16),
        compiler_params=_params("parallel", "arbitrary", "arbitrary"),
        name="dft_positions",
    )(cs, ss, p3, p3)
    return out.reshape(B * S, C)


def _merge_body(scale_ref, o_ref, f_ref, h_ref, wa_ref, wf_ref, wg0_ref, wg1_ref, b0_ref, b1_ref, out_ref,
                h8_scr):
    @pl.when(pl.program_id(1) == 0)
    def _():
        h8_scr[...] = (h_ref[...].astype(F32) * scale_ref[0]).astype(F8)

    h8 = h8_scr[...]
    a_br = jnp.dot(o_ref[...], wa_ref[...], preferred_element_type=F32)
    f_br = jnp.dot(f_ref[...], wf_ref[...], preferred_element_type=F32)
    z0 = jnp.dot(h8, wg0_ref[...], preferred_element_type=F32) * scale_ref[1] + b0_ref[...]
    z1 = jnp.dot(h8, wg1_ref[...], preferred_element_type=F32) * scale_ref[1] + b1_ref[...]
    out_ref[...] = (jax.nn.sigmoid(z0) * a_br + jax.nn.sigmoid(z1) * f_br).astype(out_ref.dtype)


def branch_merge(o, fm, h, wa, wf, wg, bg, scales, tm=512, tn=512):
    T, D = h.shape
    tm = _tile(T, tm)
    tn = _tile(D, tn)
    nj = D // tn
    ka = o.shape[1]
    kf = fm.shape[1]
    bg2 = bg.reshape(1, 2 * D).astype(F32)
    return pl.pallas_call(
        _merge_body,
        grid=(T // tm, nj),
        in_specs=[pl.BlockSpec(memory_space=pltpu.SMEM),
                  pl.BlockSpec((tm, ka), lambda i, j: (i, 0)),
                  pl.BlockSpec((tm, kf), lambda i, j: (i, 0)),
                  pl.BlockSpec((tm, D), lambda i, j: (i, 0)),
                  pl.BlockSpec((ka, tn), lambda i, j: (0, j)),
                  pl.BlockSpec((kf, tn), lambda i, j: (0, j)),
                  pl.BlockSpec((D, tn), lambda i, j: (0, j)),
                  pl.BlockSpec((D, tn), lambda i, j: (0, nj + j)),
                  pl.BlockSpec((1, tn), lambda i, j: (0, j)),
                  pl.BlockSpec((1, tn), lambda i, j: (0, nj + j))],
        out_specs=pl.BlockSpec((tm, tn), lambda i, j: (i, j)),
        out_shape=jax.ShapeDtypeStruct((T, D), BF16),
        scratch_shapes=[pltpu.VMEM((tm, D), F8)],
        compiler_params=_params("parallel", "arbitrary"),
        name="branch_merge",
    )(scales, o, fm, h, wa, wf, wg, wg, bg2, bg2)


def _cross_attn_body(q_ref, kv_ref, o_ref):
    w = CA_HEADS * CA_HEAD_DIM
    for hh in range(CA_HEADS):
        sl = slice(hh * CA_HEAD_DIM, (hh + 1) * CA_HEAD_DIM)
        k = kv_ref[:, sl]
        v = kv_ref[:, w + hh * CA_HEAD_DIM: w + (hh + 1) * CA_HEAD_DIM]
        s = lax.dot_general(q_ref[:, sl], k, (((1,), (1,)), ((), ())),
                            preferred_element_type=F32)
        m = jnp.max(s, axis=-1, keepdims=True)
        p = jnp.exp(s - m)
        l = jnp.sum(p, axis=-1, keepdims=True)
        o = jnp.dot(p.astype(BF16), v, preferred_element_type=F32)
        o_ref[:, sl] = (o / l).astype(o_ref.dtype)


def cross_attention(qc, kv, B, S, M, tq=512):
    tq = _tile(S, tq)
    w = CA_HEADS * CA_HEAD_DIM
    out = pl.pallas_call(
        _cross_attn_body,
        grid=(B, S // tq),
        in_specs=[pl.BlockSpec((None, tq, w), lambda b, i: (b, i, 0)),
                  pl.BlockSpec((None, M, 2 * w), lambda b, i: (b, 0, 0))],
        out_specs=pl.BlockSpec((None, tq, w), lambda b, i: (b, i, 0)),
        out_shape=jax.ShapeDtypeStruct((B, S, w), BF16),
        compiler_params=_params("parallel", "arbitrary"),
        name="cross_attention",
    )(qc.reshape(B, S, w), kv.reshape(B, M, 2 * w))
    return out.reshape(B * S, w)


SUBLANES = 8


def _sort_network(n):
    pairs = []

    def merge(lo, length, r):
        step = r * 2
        if step < length:
            merge(lo, length, step)
            merge(lo + r, length, step)
            pairs.extend((i, i + r) for i in range(lo + r, lo + length - r, step))
        else:
            pairs.append((lo, lo + r))

    def sort(lo, length):
        if length > 1:
            half = length // 2
            sort(lo, half)
            sort(lo + half, half)
            merge(lo, length, 1)

    sort(0, 16)
    return [(i, j) for i, j in pairs if j < n]


def _compare_exchange(v, i, j):
    v[i], v[j] = jnp.maximum(v[i], v[j]), jnp.minimum(v[i], v[j])


def _bitonic_sort16(v):
    stride = 8
    while stride:
        for i in range(16):
            if not i & stride:
                _compare_exchange(v, i, i + stride)
        stride //= 2


def _merge_sublanes(v, shift, n_valid=16):
    def other(i):
        return pltpu.roll(v[i], shift, 0)

    out = []
    for i in range(16):
        j = 15 - i
        if i < n_valid and j < n_valid:
            out.append(jnp.maximum(v[i], other(j)))
        elif i < n_valid:
            out.append(v[i])
        else:
            out.append(other(j))
    return out


def _top16_sorted(s):
    v = [s[r * SUBLANES:(r + 1) * SUBLANES, :] for r in range(16)]
    for i, j in _sort_network(16):
        _compare_exchange(v, i, j)
    for shift in (4, 2, 1):
        v = _merge_sublanes(v, shift)
        _bitonic_sort16(v)
    return v


def _pack_sublanes(vals):
    row = lax.broadcasted_iota(jnp.int32, vals[0].shape, 0)
    out = vals[SUBLANES - 1]
    for r in range(SUBLANES - 2, -1, -1):
        out = jnp.where(row == r, vals[r], out)
    return out


def _pair_threshold(t1, t2):
    p_lo = _pack_sublanes(t2[:SUBLANES])
    p_hi = _pack_sublanes(t2[SUBLANES:])
    q_hi = _pack_sublanes(t1[SUBLANES:])
    cand = [t1[0] + p_lo, t1[0] + p_hi] + [t1[a] + p_lo for a in range(1, SUBLANES)] + [q_hi + t2[0]]
    v = list(cand)
    n = len(v)
    for i, j in _sort_network(n):
        _compare_exchange(v, i, j)
    v = _merge_sublanes(v, 4, n_valid=n)
    _bitonic_sort16(v)
    v = _merge_sublanes(v, 2)
    _bitonic_sort16(v)
    v = _merge_sublanes(v, 1)
    tau = functools.reduce(jnp.minimum, v)
    return tau, cand


def _peer_route_body(hscale_ref, x_ref, g_ref, wq_ref, keys_ref,
                     s1_ref, s2_ref, e1_ref, e2_ref, tau_ref, h8_ref, q_scr, *, tl):
    x = x_ref[...]
    ms = jnp.mean(x * x, axis=-1, keepdims=True)
    h = (x * lax.rsqrt(ms + EPS) * g_ref[...]).astype(BF16)
    h8_ref[...] = (h.astype(F32) * hscale_ref[0]).astype(F8)
    q_scr[...] = lax.dot_general(wq_ref[...], h, (((1,), (1,)), ((), ())),
                                 preferred_element_type=F32)
    tm = x_ref.shape[0]
    k1 = keys_ref[0]
    k2 = keys_ref[1]

    def per_head(hh, carry):
        r0 = pl.multiple_of(hh * (2 * PEER_HALF), 2 * PEER_HALF)
        for lc in range(tm // tl):
            ls = slice(lc * tl, (lc + 1) * tl)
            s1 = jnp.dot(k1, q_scr[pl.ds(r0, PEER_HALF), ls], preferred_element_type=F32,
                         precision=lax.Precision.HIGHEST)
            s2 = jnp.dot(k2, q_scr[pl.ds(r0 + PEER_HALF, PEER_HALF), ls],
                         preferred_element_type=F32, precision=lax.Precision.HIGHEST)
            t1 = _top16_sorted(s1)
            t2 = _top16_sorted(s2)
            tau, cand = _pair_threshold(t1, t2)
            m1 = t1[0][:1]
            m2 = t2[0][:1]
            top = t1[0] + t2[0]
            zs = [jnp.where(c >= tau, jnp.exp(c - top), 0.0) for c in cand]
            z = jnp.sum(functools.reduce(lambda x, y: x + y, zs), axis=0, keepdims=True)
            s1_ref[hh, :, ls] = s1
            s2_ref[hh, :, ls] = s2
            e1_ref[hh, :, ls] = jnp.exp(s1 - m1)
            e2_ref[hh, :, ls] = jnp.exp(s2 - m2) / z
            tau_ref[hh, :, ls] = tau[:1]
        return carry

    lax.fori_loop(0, PEER_HEADS, per_head, 0)


def peer_route(x2d, gain, h_scale, wq_t, keys, tm=256, tl=128):
    T, D = x2d.shape
    tm = _tile(T, tm)
    tl = _tile(tm, tl)
    qw = wq_t.shape[0]
    big = jax.ShapeDtypeStruct((PEER_HEADS, N_KEYS, T), F32)
    big_spec = pl.BlockSpec((PEER_HEADS, N_KEYS, tm), lambda i: (0, 0, i))
    return pl.pallas_call(
        functools.partial(_peer_route_body, tl=tl),
        grid=(T // tm,),
        in_specs=[pl.BlockSpec(memory_space=pltpu.SMEM),
                  pl.BlockSpec((tm, D), lambda i: (i, 0)),
                  pl.BlockSpec((1, D), lambda i: (0, 0)),
                  pl.BlockSpec((qw, D), lambda i: (0, 0)),
                  pl.BlockSpec((2, N_KEYS, PEER_HALF), lambda i: (0, 0, 0))],
        out_specs=[big_spec, big_spec, big_spec, big_spec,
                   pl.BlockSpec((PEER_HEADS, 1, tm), lambda i: (0, 0, i)),
                   pl.BlockSpec((tm, D), lambda i: (i, 0))],
        out_shape=[big, big, big, big, jax.ShapeDtypeStruct((PEER_HEADS, 1, T), F32),
                   jax.ShapeDtypeStruct((T, D), F8)],
        scratch_shapes=[pltpu.VMEM((qw, tm), F32)],
        compiler_params=_params("parallel"),
        name="peer_route",
    )(h_scale.reshape(1).astype(F32), x2d, gain.reshape(1, D).astype(F32), wq_t, keys)


def _peer_expert_body(scale_ref, h_ref, u_ref, v_ref, s1_ref, s2_ref, e1_ref, e2_ref, tau_ref, o_ref,
                      a0, a1, acc, *, nc, nk, n_chunks):
    s = pl.program_id(0)

    @pl.when(s == 0)
    def _():
        for ref in (a0, a1):
            ref[...] = jnp.zeros_like(ref)

    @pl.when((s == 0) | ((s >= 1) & ((s - 1) % nk == 0)))
    def _():
        acc[...] = jnp.zeros_like(acc)

    ec, tm = a0.shape
    d_model = acc.shape[1]
    chunk = jnp.clip(s - 1, 0, n_chunks - 1) % nk
    tok_w = min(tm, 2 * LANES)
    dcol_w = min(d_model, 4 * LANES)
    act_scale = scale_ref[0] * (0.5 * PEER_W_SCALE)
    erf_scale = scale_ref[0] * INV_SQRT2

    def project_piece(a_wr, tc):
        cols = slice(tc * tok_w, (tc + 1) * tok_w)
        a_wr[:, cols] = lax.dot_general(u_ref[...], h_ref[cols, :], (((1,), (1,)), ((), ())),
                                        preferred_element_type=F32)

    def gate_piece(a_rd, cc, tl):
        c = chunk * nc + cc
        rows = slice(cc * N_KEYS, (cc + 1) * N_KEYS)
        cols = slice(tl * LANES, (tl + 1) * LANES)
        g = None
        for hh in range(PEER_HEADS):
            s1c = s1_ref[hh, pl.ds(c, 1), :][:, cols]
            e1c = e1_ref[hh, pl.ds(c, 1), :][:, cols]
            mask = (s2_ref[hh, :, cols] + s1c) >= tau_ref[hh, :, cols]
            term = jnp.where(mask, e2_ref[hh, :, cols] * e1c, 0.0)
            g = term if g is None else g + term
        a = a_rd[rows, cols]
        act = (a * act_scale) * (1.0 + lax.erf(a * erf_scale))
        wv = lax.clamp(-F8_MAX, act * g, F8_MAX)
        return wv.T.astype(F8)

    def apply_piece(half, w_half, dc):
        tok = slice(half * tok_w, (half + 1) * tok_w)
        cols = slice(dc * dcol_w, (dc + 1) * dcol_w)
        acc[tok, cols] += jnp.dot(w_half, v_ref[:, cols], preferred_element_type=F32)

    def step(a_wr, a_rd):
        lanes_per_half = tok_w // LANES
        for tc in range(tm // tok_w):
            project_piece(a_wr, tc)
        for half in range(tm // tok_w):
            tiles = [[gate_piece(a_rd, cc, half * lanes_per_half + tl) for cc in range(nc)]
                     for tl in range(lanes_per_half)]
            w_half = jnp.concatenate([jnp.concatenate(row, axis=1) for row in tiles], axis=0)
            for dc in range(d_model // dcol_w):
                apply_piece(half, w_half, dc)

    @pl.when(s % 2 == 0)
    def _():
        step(a0, a1)

    @pl.when(s % 2 == 1)
    def _():
        step(a1, a0)

    @pl.when((s >= 1) & ((s - 1) % nk == nk - 1))
    def _():
        o_ref[...] = (acc[...] * scale_ref[1]).astype(o_ref.dtype)


def _pow2_scale(x, target):
    m = jnp.max(jnp.abs(x)).astype(F32)
    p = jnp.exp2(jnp.floor(jnp.log2(target / jnp.maximum(m, jnp.finfo(F32).tiny))))
    return jnp.where(m > 0, jnp.clip(p, 2.0 ** -60, 2.0 ** 60), 1.0)


def peer_experts(hf8, u8, v8, scales, s1, s2, e1, e2, tau, tm=512, ec=512):
    T, D = hf8.shape
    NE = u8.shape[0]
    tm = _tile(T, tm)
    ec = _tile(NE, ec)
    nc = ec // N_KEYS
    nk = NE // ec
    n_chunks = (T // tm) * nk
    last = n_chunks - 1

    def lag(s, d):
        return jnp.clip(s - d, 0, last)

    big_spec = pl.BlockSpec((PEER_HEADS, N_KEYS, tm), lambda s: (0, 0, lag(s, 1) // nk))
    return pl.pallas_call(
        functools.partial(_peer_expert_body, nc=nc, nk=nk, n_chunks=n_chunks),
        grid=(n_chunks + 1,),
        in_specs=[pl.BlockSpec(memory_space=pltpu.SMEM),
                  pl.BlockSpec((tm, D), lambda s: (lag(s, 0) // nk, 0)),
                  pl.BlockSpec((ec, D), lambda s: (lag(s, 0) % nk, 0)),
                  pl.BlockSpec((ec, D), lambda s: (lag(s, 1) % nk, 0)),
                  big_spec, big_spec, big_spec, big_spec,
                  pl.BlockSpec((PEER_HEADS, 1, tm), lambda s: (0, 0, lag(s, 1) // nk))],
        out_specs=pl.BlockSpec((tm, D), lambda s: (lag(s, 1) // nk, 0)),
        out_shape=jax.ShapeDtypeStruct((T, D), BF16),
        scratch_shapes=[pltpu.VMEM((ec, tm), F32), pltpu.VMEM((ec, tm), F32),
                        pltpu.VMEM((tm, D), F32)],
        compiler_params=_params("arbitrary"),
        name="peer_experts",
    )(scales, hf8, u8, v8, s1, s2, e1, e2, tau)


def _trunk(x, mem, w):
    B, S, D = x.shape
    M = mem.shape[1]
    T = B * S
    C = w["four_cols"]
    x2d = x.reshape(T, D)

    h = rmsnorm_rows(x2d, w["norm_mix"], BF16)
    proj = matmul(h, w["w_in"], BF16, name="in_proj")

    cos, sin = rope_tables(S)
    k_col0 = C + N_HEADS * HEAD_DIM
    kk = qk_prep(proj, k_col0, w["k_gain"], cos, sin, S)
    o = gqa_attention(proj, kk, w["q_gain"], cos, sin, C, k_col0 + N_KV_HEADS * HEAD_DIM, B, S)

    cs, ss = w["dft_tables"][S] if S in w["dft_tables"] else dft_tables(S)
    p = matmul(proj, w["dft_ch"], BF16, name="dft_channels")
    fm = dft_positions(cs, ss, p, B, S, C, 1.0 / math.sqrt(S * C))

    merged = branch_merge(o, fm, h, w["w_attn_br"], w["w_four_br"], w["w_gate"], w["b_gate"],
                          w["gate_scales"])
    x1 = matmul(merged, w["w_out"], F32, residual=x2d, name="out_proj")

    hc = rmsnorm_rows(x1, w["norm_ca"], BF16)
    qc = matmul(hc, w["w_cq"], BF16, scale=CA_HEAD_DIM ** -0.5, name="ca_q")
    kv = norm_matmul(mem.reshape(B * M, D), w["mem_norm"], w["w_ckv"], BF16, name="ca_kv")
    oc = cross_attention(qc, kv, B, S, M)
    x2 = matmul(oc, w["w_co"], F32, residual=x1, name="ca_out")

    s1, s2, e1, e2, tau, hf8 = peer_route(x2, w["norm_ffn"], w["peer_h_scale"], w["w_pq_t"], w["sub_keys"])
    po = peer_experts(hf8, w["expert_u"], w["expert_v"], w["peer_scales"], s1, s2, e1, e2, tau)
    y = add_rmsnorm_rows(x2, po, w["final_norm"], F32)
    return y.reshape(B, S, D)


def kernel(x_prompt, x_sample, mem_prompt, mem_sample, norm_mix, w_in, q_norm, k_norm, w_attn_br, w_four_br, w_gate, b_gate, w_out, norm_ca, mem_norm, w_cq, w_ckv, w_co, norm_ffn, w_pq, sub_keys, expert_u, expert_v, final_norm):
    assert norm_mix.shape[0] == 1, "single-layer trunk"
    C = w_four_br.shape[1]
    attn_w = N_HEADS * HEAD_DIM
    kv_w = N_KV_HEADS * HEAD_DIM
    wi = w_in[0]
    w_in_r = jnp.concatenate([wi[:, attn_w + 2 * kv_w:], wi[:, :attn_w + 2 * kv_w]], axis=1).astype(BF16)
    cc, sc = dft_tables(C)
    dft_ch = jnp.concatenate([cc, sc], axis=1)
    scale = HEAD_DIM ** -0.5
    q_gain = (q_norm[0] * scale).reshape(1, HEAD_DIM).astype(F32)
    k_gain = jnp.tile(k_norm[0], N_KV_HEADS).reshape(1, -1).astype(F32)
    g_scale = _pow2_scale(w_gate[0], F8_OPERAND_TARGET)
    hm_scale = _pow2_scale(norm_mix[0] * math.sqrt(norm_mix.shape[1]), F8_OPERAND_TARGET)
    u_scale = _pow2_scale(expert_u[0], F8_OPERAND_TARGET)
    v_scale = _pow2_scale(expert_v[0], F8_OPERAND_TARGET)
    h_scale = _pow2_scale(norm_ffn[0] * math.sqrt(norm_ffn.shape[1]), F8_OPERAND_TARGET)
    w = dict(
        four_cols=C,
        norm_mix=norm_mix[0], w_in=w_in_r, q_gain=q_gain, k_gain=k_gain, dft_ch=dft_ch, dft_tables={C: (cc, sc)},
        w_attn_br=w_attn_br[0].astype(BF16), w_four_br=w_four_br[0].astype(BF16),
        w_gate=(w_gate[0] * g_scale).astype(F8), b_gate=b_gate[0], w_out=w_out[0].astype(BF16),
        gate_scales=jnp.stack([hm_scale, 1.0 / (hm_scale * g_scale)]).astype(F32),
        norm_ca=norm_ca[0], mem_norm=mem_norm[0], w_cq=w_cq[0].astype(BF16),
        w_ckv=w_ckv[0].astype(BF16), w_co=w_co[0].astype(BF16), norm_ffn=norm_ffn[0],
        w_pq_t=w_pq[0].T.astype(BF16), sub_keys=sub_keys[0].astype(F32),
        expert_u=(expert_u[0] * u_scale).astype(F8), expert_v=(expert_v[0] * v_scale).astype(F8),
        peer_h_scale=h_scale,
        peer_scales=jnp.stack([1.0 / (u_scale * h_scale), 1.0 / (PEER_W_SCALE * v_scale)]).astype(F32),
        final_norm=final_norm,
    )
    y_prompt = _trunk(x_prompt, mem_prompt, w)
    y_sample = _trunk(x_sample, mem_sample, w)
    return (y_prompt, y_sample)
```
